```python
import jax, jax.numpy as jnp
from jax import lax
import numpy as np

D_MODEL = 1024
BATCH = 4
SEQ = 4096
DEPTH = 4
DEC_BATCH = 128
DEC_SEQ = 1
PAST_LEN = 2048
PAGE_SIZE = 128

N_MIXERS = 2
N_A_LAYERS = (DEPTH + 1) // 2
N_B_LAYERS = DEPTH // 2
CHUNK = 128
D_SGU = D_MODEL
SGU_GROUPS = 8
SGU_GROUP_DIM = D_SGU // SGU_GROUPS
N_HEADS = 8
HEAD_DIM = D_MODEL // N_HEADS
N_KV_HEADS = 2
KV_GROUP = N_HEADS // N_KV_HEADS
N_IDX_HEADS = 8
IDX_DIM = 64
TOPK_MAX = 256
Q_BLOCK = 128
Q_COLS = N_HEADS * HEAD_DIM
KV_COLS = N_KV_HEADS * HEAD_DIM
QI_COLS = N_IDX_HEADS * IDX_DIM
PROJ_B = Q_COLS + 2 * KV_COLS + QI_COLS + IDX_DIM + N_IDX_HEADS
D_FF = 2816
CONV_W = 3
ALPHA = (2 * DEPTH) ** 0.25
BETA = (8 * DEPTH) ** -0.25
LN_EPS = 1e-5
N_MOD = 6

kernel_name = 'hybrid_sgu_dsa_convglu_step'


def layer_norm(x, g, b):
    xf = x.astype(jnp.float32)
    mu = jnp.mean(xf, axis=-1, keepdims=True)
    var = jnp.mean(jnp.square(xf - mu), axis=-1, keepdims=True)
    return ((xf - mu) * lax.rsqrt(var + LN_EPS)).astype(x.dtype) * g + b


def ada_params(c, w_ada, b_ada):
    return (jax.nn.silu(c) @ w_ada + b_ada).reshape(c.shape[0], N_MOD, D_MODEL)


def modulate(x, shift, scale):
    return x * (1 + scale[:, None, :]) + shift[:, None, :]


def post_norm(x, y, gate, g, b):
    return layer_norm(ALPHA * x + (1 + gate[:, None, :]) * y, g, b)


def sgu_mixer(h, w_in, b_in, norm_g, norm_b, w_s, b_s, w_out):
    z = jax.nn.gelu(h @ w_in + b_in)
    u, v = jnp.split(z, 2, axis=-1)
    v = layer_norm(v, norm_g, norm_b)
    bsz, t, _ = v.shape
    c = min(t, CHUNK)
    n = -(-t // c)
    pad = n * c - t
    vg = jnp.pad(v, ((0, 0), (0, pad), (0, 0))).reshape(bsz, n, c, SGU_GROUPS, SGU_GROUP_DIM)
    w = jnp.tril(w_s[:, :c, :c])
    mixed = jnp.einsum('gts,bnsgd->bntgd', w, vg) + b_s[:, :c].T[None, None, :, :, None]
    mixed = mixed.reshape(bsz, n * c, D_SGU)[:, :t]
    return (u * mixed) @ w_out, v


def dsa_project(h, w_in):
    bsz, t, _ = h.shape
    splits = [Q_COLS, Q_COLS + KV_COLS, Q_COLS + 2 * KV_COLS, Q_COLS + 2 * KV_COLS + QI_COLS,
              Q_COLS + 2 * KV_COLS + QI_COLS + IDX_DIM]
    q, k, v, qi, ki, wi = jnp.split(h @ w_in, splits, axis=-1)
    return (q.reshape(bsz, t, N_HEADS, HEAD_DIM),
            k.reshape(bsz, t, N_KV_HEADS, HEAD_DIM),
            v.reshape(bsz, t, N_KV_HEADS, HEAD_DIM),
            qi.reshape(bsz, t, N_IDX_HEADS, IDX_DIM),
            ki,
            wi * N_IDX_HEADS ** -0.5)


def dsa_attend(q, qi, wi, q_pos, k, v, ki, kv_pos, n_keep):
    bsz, t = q.shape[:2]
    qb = min(t, Q_BLOCK)
    nb = -(-t // qb)
    pad = nb * qb - t

    def blocks(a):
        a = jnp.pad(a, [(0, 0), (0, pad)] + [(0, 0)] * (a.ndim - 2))
        return jnp.moveaxis(a.reshape((bsz, nb, qb) + a.shape[2:]), 1, 0)

    qpos_b = jnp.pad(q_pos, (0, pad), mode='edge').reshape(nb, qb)
    ki32 = ki.astype(jnp.float32)

    def one_block(args):
        q_b, qi_b, wi_b, pos = args
        s_h = jnp.einsum('bqhd,bsd->bqhs', qi_b.astype(jnp.float32), ki32)
        s_idx = jnp.einsum('bqhs,bqh->bqs', jax.nn.relu(s_h) * IDX_DIM ** -0.5, wi_b.astype(jnp.float32))
        causal = kv_pos[None, :] <= pos[:, None]
        s_idx = jnp.where(causal[None], s_idx, -jnp.inf)
        _, sel = lax.top_k(s_idx, n_keep)
        kg = jax.vmap(lambda a, i: a[i])(k, sel)
        vg = jax.vmap(lambda a, i: a[i])(v, sel)
        sel_ok = kv_pos[sel] <= pos[None, :, None]
        qg = q_b.reshape(bsz, qb, N_KV_HEADS, KV_GROUP, HEAD_DIM)
        logits = jnp.einsum('bqhgd,bqkhd->bqhgk', qg, kg, preferred_element_type=jnp.float32) * HEAD_DIM ** -0.5
        logits = jnp.where(sel_ok[:, :, None, None, :], logits, -jnp.inf)
        p = jax.nn.softmax(logits, axis=-1).astype(vg.dtype)
        o = jnp.einsum('bqhgk,bqkhd->bqhgd', p, vg)
        return o.reshape(bsz, qb, N_HEADS * HEAD_DIM)

    out = lax.map(one_block, (blocks(q), blocks(qi), blocks(wi), qpos_b))
    return jnp.moveaxis(out, 0, 1).reshape(bsz, nb * qb, N_HEADS * HEAD_DIM)[:, :t]


def paged_rows(pool, page_table):
    g = pool[page_table]
    return g.reshape((g.shape[0], g.shape[1] * g.shape[2]) + g.shape[3:])


def conv_glu(h, w_up, conv_w, conv_b, w_down, past):
    a, u = jnp.split(h @ w_up, 2, axis=-1)
    t = a.shape[1]
    full = jnp.concatenate([past, a], axis=1)
    conv = sum(full[:, j:j + t] * conv_w[j] for j in range(CONV_W)) + conv_b
    y = (jax.nn.gelu(conv) * u) @ w_down
    return y, full[:, -(CONV_W - 1):]


def setup_inputs(seed: int = 0) -> dict:
    key = jax.random.key(seed)
    ks = list(jax.random.split(key, 40))
    cnt = [0]

    def nk():
        cnt[0] += 1
        return ks[cnt[0] - 1]

    def nrm(shape, s=1.0):
        return jax.random.normal(nk(), shape, jnp.float32) * s

    n_pages = PAST_LEN // PAGE_SIZE
    n_phys = (5 * DEC_BATCH * n_pages) // 4
    page_table = jax.random.permutation(nk(), n_phys)[:DEC_BATCH * n_pages].reshape(DEC_BATCH, n_pages).astype(jnp.int32)
    return {
        'x_prompt': nrm((BATCH, SEQ, D_MODEL)),
        'x_sample': nrm((DEC_BATCH, DEC_SEQ, D_MODEL)),
        'cache_k': nrm((N_B_LAYERS, n_phys, PAGE_SIZE, N_KV_HEADS, HEAD_DIM)),
        'cache_v': nrm((N_B_LAYERS, n_phys, PAGE_SIZE, N_KV_HEADS, HEAD_DIM)),
        'cache_kidx': nrm((N_B_LAYERS, n_phys, PAGE_SIZE, IDX_DIM)),
        'state_conv': nrm((DEPTH, DEC_BATCH, CONV_W - 1, D_FF)),
        'page_table': page_table,
        'c_prompt': nrm((BATCH, D_MODEL)),
        'c_sample': nrm((DEC_BATCH, D_MODEL)),
        'w_ada': nrm((DEPTH, D_MODEL, N_MOD * D_MODEL), 0.2 * D_MODEL ** -0.5),
        'b_ada': nrm((DEPTH, N_MOD * D_MODEL), 0.01),
        'ln_g': 1.0 + nrm((DEPTH, 2, D_MODEL), 0.01),
        'ln_b': nrm((DEPTH, 2, D_MODEL), 0.01),
        'sgu_w_in': nrm((N_A_LAYERS, D_MODEL, 2 * D_SGU), D_MODEL ** -0.5),
        'sgu_b_in': nrm((N_A_LAYERS, 2 * D_SGU), 0.01),
        'sgu_norm_g': 1.0 + nrm((N_A_LAYERS, D_SGU), 0.01),
        'sgu_norm_b': nrm((N_A_LAYERS, D_SGU), 0.01),
        'sgu_w_s': nrm((N_A_LAYERS, SGU_GROUPS, CHUNK, CHUNK), CHUNK ** -0.5),
        'sgu_b_s': 1.0 + nrm((N_A_LAYERS, SGU_GROUPS, CHUNK), 0.01),
        'sgu_w_out': nrm((N_A_LAYERS, D_SGU, D_MODEL), BETA * D_SGU ** -0.5),
        'dsa_w_in': nrm((N_B_LAYERS, D_MODEL, PROJ_B), D_MODEL ** -0.5),
        'dsa_w_out': nrm((N_B_LAYERS, Q_COLS, D_MODEL), BETA * Q_COLS ** -0.5),
        'ffn_w_up': nrm((DEPTH, D_MODEL, 2 * D_FF), D_MODEL ** -0.5),
        'ffn_conv_w': nrm((DEPTH, CONV_W, D_FF), CONV_W ** -0.5),
        'ffn_conv_b': nrm((DEPTH, D_FF), 0.01),
        'ffn_w_down': nrm((DEPTH, D_FF, D_MODEL), BETA * D_FF ** -0.5),
    }


def reference(x_prompt, x_sample, cache_k, cache_v, cache_kidx, state_conv, page_table, c_prompt, c_sample,
              w_ada, b_ada, ln_g, ln_b, sgu_w_in, sgu_b_in, sgu_norm_g, sgu_norm_b, sgu_w_s, sgu_b_s, sgu_w_out,
              dsa_w_in, dsa_w_out, ffn_w_up, ffn_conv_w, ffn_conv_b, ffn_w_down):
    t_p = x_prompt.shape[1]
    t_s = x_sample.shape[1]
    past_len = page_table.shape[1] * cache_k.shape[2]
    pos_p = jnp.arange(t_p, dtype=jnp.int32)
    pos_s = past_len + jnp.arange(t_s, dtype=jnp.int32)
    kvpos_s = jnp.arange(past_len + t_s, dtype=jnp.int32)
    keep_p = min(TOPK_MAX, t_p // 4)
    keep_s = min(TOPK_MAX, (past_len + t_s) // 4)
    conv_zero = jnp.zeros((x_prompt.shape[0], CONV_W - 1, D_FF), x_prompt.dtype)

    xp, xs = x_prompt, x_sample
    kp_l, vp_l, kip_l, ks_l, vs_l, kis_l, sgu_l, convp_l, convs_l = [], [], [], [], [], [], [], [], []
    for i in range(DEPTH):
        mp = ada_params(c_prompt, w_ada[i], b_ada[i])
        ms = ada_params(c_sample, w_ada[i], b_ada[i])
        hp = modulate(xp, mp[:, 0], mp[:, 1])
        hs = modulate(xs, ms[:, 0], ms[:, 1])
        j = i // N_MIXERS
        if i % N_MIXERS == 0:
            yp, _ = sgu_mixer(hp, sgu_w_in[j], sgu_b_in[j], sgu_norm_g[j], sgu_norm_b[j], sgu_w_s[j], sgu_b_s[j], sgu_w_out[j])
            ys, v_rows = sgu_mixer(hs, sgu_w_in[j], sgu_b_in[j], sgu_norm_g[j], sgu_norm_b[j], sgu_w_s[j], sgu_b_s[j], sgu_w_out[j])
            sgu_l.append(v_rows)
        else:
            qp, kp, vp, qip, kip, wip = dsa_project(hp, dsa_w_in[j])
            yp = dsa_attend(qp, qip, wip, pos_p, kp, vp, kip, pos_p, keep_p) @ dsa_w_out[j]
            qs, ks_new, vs_new, qis, kis_new, wis = dsa_project(hs, dsa_w_in[j])
            k_all = jnp.concatenate([paged_rows(cache_k[j], page_table), ks_new], axis=1)
            v_all = jnp.concatenate([paged_rows(cache_v[j], page_table), vs_new], axis=1)
            ki_all = jnp.concatenate([paged_rows(cache_kidx[j], page_table), kis_new], axis=1)
            ys = dsa_attend(qs, qis, wis, pos_s, k_all, v_all, ki_all, kvpos_s, keep_s) @ dsa_w_out[j]
            kp_l.append(kp)
            vp_l.append(vp)
            kip_l.append(kip)
            ks_l.append(ks_new)
            vs_l.append(vs_new)
            kis_l.append(kis_new)
        xp = post_norm(xp, yp, mp[:, 2], ln_g[i, 0], ln_b[i, 0])
        xs = post_norm(xs, ys, ms[:, 2], ln_g[i, 0], ln_b[i, 0])
        fp, cp = conv_glu(modulate(xp, mp[:, 3], mp[:, 4]), ffn_w_up[i], ffn_conv_w[i], ffn_conv_b[i], ffn_w_down[i], conv_zero)
        fs, cs = conv_glu(modulate(xs, ms[:, 3], ms[:, 4]), ffn_w_up[i], ffn_conv_w[i], ffn_conv_b[i], ffn_w_down[i], state_conv[i])
        xp = post_norm(xp, fp, mp[:, 5], ln_g[i, 1], ln_b[i, 1])
        xs = post_norm(xs, fs, ms[:, 5], ln_g[i, 1], ln_b[i, 1])
        convp_l.append(cp)
        convs_l.append(cs)

    new_k_prompt = jnp.stack(kp_l)
    new_v_prompt = jnp.stack(vp_l)
    new_kidx_prompt = jnp.stack(kip_l)
    new_k_sample = jnp.stack(ks_l)
    new_v_sample = jnp.stack(vs_l)
    new_kidx_sample = jnp.stack(kis_l)
    new_sgu_v_sample = jnp.stack(sgu_l)
    new_conv_prompt = jnp.stack(convp_l)
    new_conv_sample = jnp.stack(convs_l)
    return (xp, xs, new_k_prompt, new_v_prompt, new_kidx_prompt, new_k_sample, new_v_sample, new_kidx_sample,
            new_sgu_v_sample, new_conv_prompt, new_conv_sample)
```

```python
import functools

import jax
import jax.numpy as jnp
from jax import lax
from jax.experimental import pallas as pl
from jax.experimental.pallas import tpu as pltpu

F32 = jnp.float32
BF16 = jnp.bfloat16
I32 = jnp.int32

D_MODEL = 1024
DEPTH = 4
N_MOD = 6
CHUNK = 128
D_SGU = D_MODEL
SGU_GROUPS = 8
SGU_GROUP_DIM = D_SGU // SGU_GROUPS
N_HEADS = 8
HEAD_DIM = D_MODEL // N_HEADS
N_KV_HEADS = 2
KV_GROUP = N_HEADS // N_KV_HEADS
N_IDX_HEADS = 8
IDX_DIM = 64
TOPK_MAX = 256
Q_COLS = N_HEADS * HEAD_DIM
KV_COLS = N_KV_HEADS * HEAD_DIM
QI_COLS = N_IDX_HEADS * IDX_DIM
D_FF = 2816
CONV_W = 3
ALPHA = (2 * DEPTH) ** 0.25
LN_EPS = 1e-5

LANE = 128
SUBLANE = 8
VMEM_LIMIT = 56 * 1024 * 1024

TM = 512
FF_CHUNK = 256
N_FF_CHUNKS = D_FF // FF_CHUNK
QB = 128
TK = 512
KEY_INVALID = -2 ** 31
NEG = -1e30
QI_PAD = N_IDX_HEADS * LANE
PROJ_COLS = Q_COLS + 2 * KV_COLS + QI_PAD + 2 * LANE
SB = 8


def _dot(a, b):
    return jnp.dot(a, b, preferred_element_type=F32)


def _dot_nt(a, b):
    return lax.dot_general(a, b, (((1,), (1,)), ((), ())), preferred_element_type=F32)


def _ln(x):
    mu = jnp.mean(x, axis=-1, keepdims=True)
    xc = x - mu
    var = jnp.mean(xc * xc, axis=-1, keepdims=True)
    return xc * lax.rsqrt(var + LN_EPS)


def _modulate(x, shift, scale):
    return x * (1.0 + scale) + shift


def _post_norm(x, y, gate, g, b):
    return _ln(ALPHA * x + (1.0 + gate) * y) * g + b


def _prompt_mod(mod_ref, m):
    return mod_ref[m, pl.ds(pl.program_id(0), 1), :]


def _score_key(s):
    bits = lax.bitcast_convert_type(s, I32)
    return jnp.where(bits < 0, jnp.int32(KEY_INVALID) - bits, bits)


def _kth_threshold(keys_ref, n_chunks, rows, width, kth):
    def bit_body(it, thr):
        cand = thr + jnp.left_shift(jnp.int32(1), 31 - it)

        def cnt_body(c, acc):
            kc = keys_ref[c]
            for t in range(width // LANE):
                acc = acc + jnp.where(kc[:, t * LANE:(t + 1) * LANE] >= cand, 1.0, 0.0)
            return acc

        acc = lax.fori_loop(0, n_chunks, cnt_body, jnp.zeros((rows, LANE), F32))
        cnt = jnp.sum(acc, axis=1, keepdims=True)
        return jnp.where(cnt >= kth, cand, thr)

    return lax.fori_loop(0, 32, bit_body, jnp.full((rows, LANE), KEY_INVALID, I32))


def _params(sem=None):
    return pltpu.CompilerParams(dimension_semantics=sem, vmem_limit_bytes=VMEM_LIMIT)


def _const_spec(shape):
    return pl.BlockSpec(shape, lambda *_: (0,) * len(shape), pipeline_mode=pl.Buffered(1))


def _full_spec(shape):
    return pl.BlockSpec(shape, lambda *_: (0,) * len(shape))


def _ada_kernel(cs_ref, cp_ref, w_ref, b_ref, os_ref, op_ref):
    w = w_ref[...].astype(BF16)
    bias = b_ref[...]
    os_ref[...] = _dot(jax.nn.silu(cs_ref[...]).astype(BF16), w) + bias
    op_ref[...] = _dot(jax.nn.silu(cp_ref[...]).astype(BF16), w) + bias


def _ada_call(c_sample, c_prompt8, w_ada, b_ada):
    tn = 512
    nn = D_MODEL // tn
    n_s = c_sample.shape[0]
    return pl.pallas_call(
        _ada_kernel,
        grid=(DEPTH, N_MOD, nn),
        in_specs=[
            pl.BlockSpec((n_s, D_MODEL), lambda l, m, n: (0, 0)),
            pl.BlockSpec((SUBLANE, D_MODEL), lambda l, m, n: (0, 0)),
            pl.BlockSpec((None, D_MODEL, tn), lambda l, m, n: (l, 0, m * nn + n)),
            pl.BlockSpec((None, 1, tn), lambda l, m, n: (l, 0, m * nn + n)),
        ],
        out_specs=[
            pl.BlockSpec((None, None, n_s, tn), lambda l, m, n: (l, m, 0, n)),
            pl.BlockSpec((None, None, SUBLANE, tn), lambda l, m, n: (l, m, 0, n)),
        ],
        out_shape=[
            jax.ShapeDtypeStruct((DEPTH, N_MOD, n_s, D_MODEL), F32),
            jax.ShapeDtypeStruct((DEPTH, N_MOD, SUBLANE, D_MODEL), F32),
        ],
        compiler_params=_params(("arbitrary",) * 3),
        name="ada_params",
    )(c_sample, c_prompt8, w_ada, b_ada.reshape(DEPTH, 1, N_MOD * D_MODEL))


def _sgu_front(x, shift, scale, w_in_ref, b_in_ref, ng_ref, nb_ref):
    h = _modulate(x, shift, scale).astype(BF16)
    u = jax.nn.gelu(_dot(h, w_in_ref[:, :D_SGU]) + b_in_ref[:, :D_SGU])
    v = jax.nn.gelu(_dot(h, w_in_ref[:, D_SGU:]) + b_in_ref[:, D_SGU:])
    v = _ln(v) * ng_ref[...] + nb_ref[...]
    return u, v


def _sgu_prompt_kernel(x_ref, mod_ref, lng_ref, lnb_ref, w_in_ref, b_in_ref, ng_ref, nb_ref,
                       wtril_ref, bs_ref, w_out_ref, o_ref, gated_ref):
    x = x_ref[...]
    u, v = _sgu_front(x, _prompt_mod(mod_ref, 0), _prompt_mod(mod_ref, 1),
                      w_in_ref, b_in_ref, ng_ref, nb_ref)
    vb = v.astype(BF16)
    n_chunks = x.shape[0] // CHUNK
    for g in range(SGU_GROUPS):
        cols = slice(g * SGU_GROUP_DIM, (g + 1) * SGU_GROUP_DIM)
        rhs = jnp.concatenate([vb[n * CHUNK:(n + 1) * CHUNK, cols] for n in range(n_chunks)], axis=1)
        mixed = _dot(wtril_ref[g], rhs)
        for n in range(n_chunks):
            rows = slice(n * CHUNK, (n + 1) * CHUNK)
            mix_n = mixed[:, n * SGU_GROUP_DIM:(n + 1) * SGU_GROUP_DIM] + bs_ref[:, cols]
            gated_ref[rows, cols] = (u[rows, cols] * mix_n).astype(BF16)
    y = _dot(gated_ref[...], w_out_ref[...])
    o_ref[...] = _post_norm(x, y, _prompt_mod(mod_ref, 2), lng_ref[...], lnb_ref[...])


def _sgu_sample_kernel(x_ref, mod_ref, lng_ref, lnb_ref, w_in_ref, b_in_ref, ng_ref, nb_ref,
                       ws0_ref, bs0_ref, w_out_ref, o_ref, v_ref):
    x = x_ref[...]
    u, v = _sgu_front(x, mod_ref[0], mod_ref[1], w_in_ref, b_in_ref, ng_ref, nb_ref)
    v_ref[...] = v
    mixed = v * ws0_ref[...] + bs0_ref[...]
    y = _dot((u * mixed).astype(BF16), w_out_ref[...])
    o_ref[...] = _post_norm(x, y, mod_ref[2], lng_ref[...], lnb_ref[...])


def _row(v):
    return v.reshape(1, -1)


def _sgu_prompt_call(x, mod_p, lng, lnb, w_in, b_in, ng, nb, wtril, bs_full, w_out):
    bsz, t, _ = x.shape
    return pl.pallas_call(
        _sgu_prompt_kernel,
        grid=(bsz, t // TM),
        in_specs=[
            pl.BlockSpec((None, TM, D_MODEL), lambda b, i: (b, i, 0)),
            _const_spec(mod_p.shape),
            _const_spec((1, D_MODEL)), _const_spec((1, D_MODEL)),
            _const_spec(w_in.shape), _const_spec((1, 2 * D_SGU)),
            _const_spec((1, D_SGU)), _const_spec((1, D_SGU)),
            _const_spec(wtril.shape), _const_spec(bs_full.shape), _const_spec(w_out.shape),
        ],
        out_specs=pl.BlockSpec((None, TM, D_MODEL), lambda b, i: (b, i, 0)),
        out_shape=jax.ShapeDtypeStruct(x.shape, F32),
        scratch_shapes=[pltpu.VMEM((TM, D_SGU), BF16)],
        compiler_params=_params(("arbitrary", "arbitrary")),
        name="sgu_prompt",
    )(x, mod_p, _row(lng), _row(lnb), w_in, _row(b_in), _row(ng), _row(nb), wtril, bs_full, w_out)


def _sgu_sample_call(x, mod_s, lng, lnb, w_in, b_in, ng, nb, ws0, bs0, w_out):
    n = x.shape[0]
    args = (x, mod_s, _row(lng), _row(lnb), w_in, _row(b_in), _row(ng), _row(nb), _row(ws0), _row(bs0), w_out)
    return pl.pallas_call(
        _sgu_sample_kernel,
        grid=(1,),
        in_specs=[_const_spec(a.shape) for a in args],
        out_specs=[_full_spec((n, D_MODEL)), _full_spec((n, D_SGU))],
        out_shape=[jax.ShapeDtypeStruct((n, D_MODEL), F32), jax.ShapeDtypeStruct((n, D_SGU), F32)],
        compiler_params=_params(("arbitrary",)),
        name="sgu_sample",
    )(*args)


def _ffn_prompt_kernel(x_ref, mod_ref, lng_ref, lnb_ref, w_up_ref, cw_ref, cb_ref, w_down_ref,
                       o_ref, st_ref, abuf_ref, g_ref):
    i = pl.program_id(1)
    tm = x_ref.shape[0]

    @pl.when(i == 0)
    def _():
        abuf_ref[:, 0:SUBLANE, :] = jnp.zeros((N_FF_CHUNKS, SUBLANE, FF_CHUNK), F32)

    x = x_ref[...]
    h = _modulate(x, _prompt_mod(mod_ref, 3), _prompt_mod(mod_ref, 4)).astype(BF16)
    for c in range(N_FF_CHUNKS):
        cols = slice(c * FF_CHUNK, (c + 1) * FF_CHUNK)
        ucols = slice(D_FF + c * FF_CHUNK, D_FF + (c + 1) * FF_CHUNK)
        a = _dot(h, w_up_ref[:, cols])
        abuf_ref[c, SUBLANE:tm + SUBLANE, :] = a
        a_m1 = abuf_ref[c, SUBLANE - 1:tm + SUBLANE - 1, :]
        a_m2 = abuf_ref[c, SUBLANE - 2:tm + SUBLANE - 2, :]
        conv = a_m2 * cw_ref[0:1, cols] + a_m1 * cw_ref[1:2, cols] + a * cw_ref[2:3, cols] + cb_ref[:, cols]
        abuf_ref[c, 0:SUBLANE, :] = abuf_ref[c, tm:tm + SUBLANE, :]
        u = _dot(h, w_up_ref[:, ucols])
        g_ref[:, cols] = (jax.nn.gelu(conv) * u).astype(BF16)

    @pl.when(i == pl.num_programs(1) - 1)
    def _():
        for c in range(N_FF_CHUNKS):
            st_ref[:, c * FF_CHUNK:(c + 1) * FF_CHUNK] = abuf_ref[c, SUBLANE - (CONV_W - 1):SUBLANE, :]

    y = _dot(g_ref[...], w_down_ref[...])
    o_ref[...] = _post_norm(x, y, _prompt_mod(mod_ref, 5), lng_ref[...], lnb_ref[...])


def _ffn_sample_kernel(x_ref, mod_ref, lng_ref, lnb_ref, w_up_ref, cw_ref, cb_ref, w_down_ref, past_ref,
                       o_ref, a_ref, g_ref):
    x = x_ref[...]
    h = _modulate(x, mod_ref[3], mod_ref[4]).astype(BF16)
    for c in range(N_FF_CHUNKS):
        cols = slice(c * FF_CHUNK, (c + 1) * FF_CHUNK)
        ucols = slice(D_FF + c * FF_CHUNK, D_FF + (c + 1) * FF_CHUNK)
        a = _dot(h, w_up_ref[:, cols])
        a_ref[:, cols] = a
        conv = (past_ref[0, :, cols] * cw_ref[0:1, cols] + past_ref[1, :, cols] * cw_ref[1:2, cols]
                + a * cw_ref[2:3, cols] + cb_ref[:, cols])
        u = _dot(h, w_up_ref[:, ucols])
        g_ref[:, cols] = (jax.nn.gelu(conv) * u).astype(BF16)
    y = _dot(g_ref[...], w_down_ref[...])
    o_ref[...] = _post_norm(x, y, mod_ref[5], lng_ref[...], lnb_ref[...])


def _ffn_prompt_call(x, mod_p, lng, lnb, w_up, cw, cb, w_down):
    bsz, t, _ = x.shape
    return pl.pallas_call(
        _ffn_prompt_kernel,
        grid=(bsz, t // TM),
        in_specs=[
            pl.BlockSpec((None, TM, D_MODEL), lambda b, i: (b, i, 0)),
            _const_spec(mod_p.shape),
            _const_spec((1, D_MODEL)), _const_spec((1, D_MODEL)),
            _const_spec(w_up.shape), _const_spec(cw.shape), _const_spec((1, D_FF)), _const_spec(w_down.shape),
        ],
        out_specs=[
            pl.BlockSpec((None, TM, D_MODEL), lambda b, i: (b, i, 0)),
            pl.BlockSpec((None, CONV_W - 1, D_FF), lambda b, i: (b, 0, 0)),
        ],
        out_shape=[jax.ShapeDtypeStruct(x.shape, F32), jax.ShapeDtypeStruct((bsz, CONV_W - 1, D_FF), F32)],
        scratch_shapes=[pltpu.VMEM((N_FF_CHUNKS, TM + SUBLANE, FF_CHUNK), F32), pltpu.VMEM((TM, D_FF), BF16)],
        compiler_params=_params(("arbitrary", "arbitrary")),
        name="ffn_prompt",
    )(x, mod_p, _row(lng), _row(lnb), w_up, cw, _row(cb), w_down)


def _ffn_sample_call(x, mod_s, lng, lnb, w_up, cw, cb, w_down, past_t):
    n = x.shape[0]
    args = (x, mod_s, _row(lng), _row(lnb), w_up, cw, _row(cb), w_down, past_t)
    return pl.pallas_call(
        _ffn_sample_kernel,
        grid=(1,),
        in_specs=[_const_spec(a.shape) for a in args],
        out_specs=[_full_spec((n, D_MODEL)), _full_spec((n, D_FF))],
        out_shape=[jax.ShapeDtypeStruct((n, D_MODEL), F32), jax.ShapeDtypeStruct((n, D_FF), F32)],
        scratch_shapes=[pltpu.VMEM((n, D_FF), BF16)],
        compiler_params=_params(("arbitrary",)),
        name="ffn_sample",
    )(*args)


_C_K = Q_COLS
_C_V = _C_K + KV_COLS
_C_QI = _C_V + KV_COLS
_C_KI = _C_QI + QI_PAD
_C_WI = _C_KI + LANE


def _proj_common(h, w_ref):
    q = _dot(h, w_ref[:, 0:_C_K]) * HEAD_DIM ** -0.5
    k = _dot(h, w_ref[:, _C_K:_C_V])
    v = _dot(h, w_ref[:, _C_V:_C_QI])
    qi = _dot(h, w_ref[:, _C_QI:_C_KI])
    ki = _dot(h, w_ref[:, _C_KI:_C_WI])
    wi = _dot(h, w_ref[:, _C_WI:PROJ_COLS]) * N_IDX_HEADS ** -0.5 * IDX_DIM ** -0.5
    return q, k, v, qi, ki, wi


def _store_kv_rows(ref, x):
    n = x.shape[0]
    for g in range(N_KV_HEADS):
        ref[pl.ds(g, n, stride=N_KV_HEADS), :] = x[:, g * HEAD_DIM:(g + 1) * HEAD_DIM]


def _proj_prompt_kernel(x_ref, mod_ref, w_ref, q_ref, k_ref, v_ref, kb_ref, vb_ref, qi_ref, ki_ref, kib_ref, wi_ref):
    h = _modulate(x_ref[...], _prompt_mod(mod_ref, 0), _prompt_mod(mod_ref, 1)).astype(BF16)
    q, k, v, qi, ki, wi = _proj_common(h, w_ref)
    q_ref[...] = q.astype(BF16)
    _store_kv_rows(k_ref, k)
    _store_kv_rows(v_ref, v)
    kb_ref[...] = k.astype(BF16)
    vb_ref[...] = v.astype(BF16)
    for hh in range(N_IDX_HEADS):
        qi_ref[hh] = qi[:, hh * LANE:(hh + 1) * LANE].astype(BF16)
    ki_ref[...] = ki[:, :IDX_DIM]
    kib_ref[...] = ki.astype(BF16)
    wi_ref[...] = wi


def _proj_sample_kernel(x_ref, mod_ref, w_ref, q_ref, k_ref, v_ref, qi_ref, ki_ref, wi_ref, sn_ref):
    h = _modulate(x_ref[...], mod_ref[0], mod_ref[1]).astype(BF16)
    q, k, v, qi, ki, wi = _proj_common(h, w_ref)
    q_ref[...] = q
    _store_kv_rows(k_ref, k)
    _store_kv_rows(v_ref, v)
    qi_ref[...] = qi
    ki_ref[...] = ki[:, :IDX_DIM]
    wi_ref[...] = wi
    kr = ki.astype(BF16).astype(F32)
    sn = jnp.zeros((x_ref.shape[0], 1), F32)
    for hh in range(N_IDX_HEADS):
        qr = qi[:, hh * LANE:(hh + 1) * LANE].astype(BF16).astype(F32)
        sh = jnp.sum(qr * kr, axis=1, keepdims=True)
        sn = sn + jnp.maximum(sh, 0.0) * wi[:, hh:hh + 1]
    sn_ref[...] = jnp.broadcast_to(sn, sn_ref.shape)


def _proj_prompt_call(x, mod_p, w_proj):
    bsz, t, _ = x.shape
    tok = lambda b, i: (b, i, 0)
    return pl.pallas_call(
        _proj_prompt_kernel,
        grid=(bsz, t // TM),
        in_specs=[pl.BlockSpec((None, TM, D_MODEL), tok), _const_spec(mod_p.shape), _const_spec(w_proj.shape)],
        out_specs=[
            pl.BlockSpec((None, TM, Q_COLS), tok),
            pl.BlockSpec((None, N_KV_HEADS * TM, HEAD_DIM), tok), pl.BlockSpec((None, N_KV_HEADS * TM, HEAD_DIM), tok),
            pl.BlockSpec((None, TM, KV_COLS), tok), pl.BlockSpec((None, TM, KV_COLS), tok),
            pl.BlockSpec((None, N_IDX_HEADS, TM, LANE), lambda b, i: (b, 0, i, 0)),
            pl.BlockSpec((None, TM, IDX_DIM), tok),
            pl.BlockSpec((None, TM, LANE), tok),
            pl.BlockSpec((None, TM, LANE), tok),
        ],
        out_shape=[
            jax.ShapeDtypeStruct((bsz, t, Q_COLS), BF16),
            jax.ShapeDtypeStruct((bsz, N_KV_HEADS * t, HEAD_DIM), F32),
            jax.ShapeDtypeStruct((bsz, N_KV_HEADS * t, HEAD_DIM), F32),
            jax.ShapeDtypeStruct((bsz, t, KV_COLS), BF16), jax.ShapeDtypeStruct((bsz, t, KV_COLS), BF16),
            jax.ShapeDtypeStruct((bsz, N_IDX_HEADS, t, LANE), BF16),
            jax.ShapeDtypeStruct((bsz, t, IDX_DIM), F32),
            jax.ShapeDtypeStruct((bsz, t, LANE), BF16),
            jax.ShapeDtypeStruct((bsz, t, LANE), F32),
        ],
        compiler_params=_params(("arbitrary", "arbitrary")),
        name="dsa_proj_prompt",
    )(x, mod_p, w_proj)


def _proj_sample_call(x, mod_s, w_proj):
    n = x.shape[0]
    args = (x, mod_s, w_proj)
    kv_shape = (N_KV_HEADS * n, HEAD_DIM)
    shapes = ((n, Q_COLS), kv_shape, kv_shape, (n, QI_PAD), (n, IDX_DIM), (n, LANE), (n, LANE))
    return pl.pallas_call(
        _proj_sample_kernel,
        grid=(1,),
        in_specs=[_const_spec(a.shape) for a in args],
        out_specs=[_full_spec(s) for s in shapes],
        out_shape=[jax.ShapeDtypeStruct(s, F32) for s in shapes],
        compiler_params=_params(("arbitrary",)),
        name="dsa_proj_sample",
    )(*args)


def _attend_prompt_kernel(x_ref, mod_ref, lng_ref, lnb_ref, q_ref, qi_ref, wi_ref, kb_ref, vb_ref, kib_ref,
                          w_out_ref, o_ref, keys_ref, q4_ref, wb_ref, m_ref, l_ref, acc_ref):
    i = pl.program_id(1)
    n_chunks = (i * QB) // TK + 1
    lane_tiles = TK // LANE

    for hh in range(N_IDX_HEADS):
        wb_ref[hh] = jnp.broadcast_to(wi_ref[:, hh:hh + 1], (QB, LANE))
    for g in range(N_KV_HEADS):
        for hh in range(KV_GROUP):
            head = g * KV_GROUP + hh
            q4_ref[g, hh * QB:(hh + 1) * QB, :] = q_ref[:, head * HEAD_DIM:(head + 1) * HEAD_DIM]

    q_pos = i * QB + lax.broadcasted_iota(I32, (QB, TK), 0)
    k_lane = lax.broadcasted_iota(I32, (QB, TK), 1)
    qs = qi_ref[...].reshape(N_IDX_HEADS * QB, LANE)

    def score_body(c, carry):
        start = pl.multiple_of(c * TK, TK)
        s = _dot_nt(qs, kib_ref[pl.ds(start, TK), :])
        sidx = jnp.zeros((QB, TK), F32)
        for hh in range(N_IDX_HEADS):
            w_h = jnp.concatenate([wb_ref[hh]] * lane_tiles, axis=1)
            sidx = sidx + jnp.maximum(s[hh * QB:(hh + 1) * QB, :], 0.0) * w_h
        keys_ref[c] = jnp.where(k_lane + start <= q_pos, _score_key(sidx), KEY_INVALID)
        return carry

    lax.fori_loop(0, n_chunks, score_body, 0)

    thr = _kth_threshold(keys_ref, n_chunks, QB, TK, TOPK_MAX)
    thr_t = jnp.concatenate([thr] * lane_tiles, axis=1)

    m_ref[...] = jnp.full(m_ref.shape, NEG, F32)
    l_ref[...] = jnp.zeros(l_ref.shape, F32)
    acc_ref[...] = jnp.zeros(acc_ref.shape, F32)

    def att_body(c, carry):
        start = pl.multiple_of(c * TK, TK)
        kc = keys_ref[c]
        bias = jnp.where((kc >= thr_t) & (kc != KEY_INVALID), 0.0, NEG)
        bias4 = jnp.concatenate([bias] * KV_GROUP, axis=0)
        for g in range(N_KV_HEADS):
            cols = slice(g * HEAD_DIM, (g + 1) * HEAD_DIM)
            s = _dot_nt(q4_ref[g], kb_ref[pl.ds(start, TK), cols]) + bias4
            m_prev = m_ref[g]
            m_new = jnp.maximum(m_prev, jnp.max(s, axis=1, keepdims=True))
            alpha = jnp.exp(m_prev - m_new)
            p = jnp.exp(s - jnp.concatenate([m_new] * lane_tiles, axis=1))
            l_ref[g] = alpha * l_ref[g] + jnp.sum(p, axis=1, keepdims=True)
            acc_ref[g] = alpha * acc_ref[g] + _dot(p.astype(BF16), vb_ref[pl.ds(start, TK), cols])
            m_ref[g] = m_new
        return carry

    lax.fori_loop(0, n_chunks, att_body, 0)

    heads = []
    for g in range(N_KV_HEADS):
        og = acc_ref[g] / l_ref[g]
        heads += [og[hh * QB:(hh + 1) * QB, :] for hh in range(KV_GROUP)]
    o = jnp.concatenate(heads, axis=1).astype(BF16)
    y = _dot(o, w_out_ref[...])
    o_ref[...] = _post_norm(x_ref[...], y, _prompt_mod(mod_ref, 2), lng_ref[...], lnb_ref[...])


def _attend_prompt_call(x, mod_p, lng, lnb, q, qi_hm, wi, kb, vb, kib, w_out):
    bsz, t, _ = x.shape
    tok = lambda b, i: (b, i, 0)
    seq = lambda b, i: (b, 0, 0)
    return pl.pallas_call(
        _attend_prompt_kernel,
        grid=(bsz, t // QB),
        in_specs=[
            pl.BlockSpec((None, QB, D_MODEL), tok),
            _const_spec(mod_p.shape), _const_spec((1, D_MODEL)), _const_spec((1, D_MODEL)),
            pl.BlockSpec((None, QB, Q_COLS), tok),
            pl.BlockSpec((None, N_IDX_HEADS, QB, LANE), lambda b, i: (b, 0, i, 0)),
            pl.BlockSpec((None, QB, LANE), tok),
            pl.BlockSpec((None, t, KV_COLS), seq), pl.BlockSpec((None, t, KV_COLS), seq),
            pl.BlockSpec((None, t, LANE), seq),
            _const_spec(w_out.shape),
        ],
        out_specs=pl.BlockSpec((None, QB, D_MODEL), tok),
        out_shape=jax.ShapeDtypeStruct(x.shape, F32),
        scratch_shapes=[
            pltpu.VMEM((t // TK, QB, TK), I32),
            pltpu.VMEM((N_KV_HEADS, KV_GROUP * QB, HEAD_DIM), BF16),
            pltpu.VMEM((N_IDX_HEADS, QB, LANE), F32),
            pltpu.VMEM((N_KV_HEADS, KV_GROUP * QB, LANE), F32),
            pltpu.VMEM((N_KV_HEADS, KV_GROUP * QB, LANE), F32),
            pltpu.VMEM((N_KV_HEADS, KV_GROUP * QB, HEAD_DIM), F32),
        ],
        compiler_params=_params(("arbitrary", "arbitrary")),
        name="dsa_attend_prompt",
    )(x, mod_p, _row(lng), _row(lnb), q, qi_hm, wi, kb, vb, kib, w_out)


def _page_copies(pt_ref, cache_ref, layer, buf_ref, sem_ref, slot, first_sample, n_samples, n_pages):
    rows = cache_ref.shape[2]
    copies = []
    for bb in range(n_samples):
        for p in range(n_pages):
            src = cache_ref.at[layer, pt_ref[first_sample + bb, p]]
            dst = buf_ref.at[slot, bb, pl.ds(p * rows, rows), :]
            copies.append(pltpu.make_async_copy(src, dst, sem_ref.at[slot]))
    return copies


def _select_sample_kernel(pt_ref, qi_ref, wbc_ref, sn_ref, cache_ref, bias_ref, buf_ref, keys_ref, sem_ref,
                          *, layer, n_pages, page):
    step = pl.program_id(0)
    slot = step % 2
    past = n_pages * page
    width = past + LANE

    def start(s, sl):
        for cp in _page_copies(pt_ref, cache_ref, layer, buf_ref, sem_ref, sl, s * SB, SB, n_pages):
            cp.start()

    @pl.when(step == 0)
    def _():
        start(0, 0)

    @pl.when(step + 1 < pl.num_programs(0))
    def _():
        start(step + 1, 1 - slot)

    for cp in _page_copies(pt_ref, cache_ref, layer, buf_ref, sem_ref, slot, step * SB, SB, n_pages):
        cp.wait()

    for bb in range(SB):
        qs = qi_ref[bb][:, :IDX_DIM].astype(BF16)
        s = _dot_nt(qs, buf_ref[slot, bb].astype(BF16))
        w = jnp.concatenate([wbc_ref[bb]] * (past // LANE), axis=1)
        sidx = jnp.sum(jnp.maximum(s, 0.0) * w, axis=0, keepdims=True)
        keys_ref[0, bb:bb + 1, 0:past] = _score_key(sidx)
    lane = lax.broadcasted_iota(I32, (SB, LANE), 1)
    keys_ref[0, :, past:width] = jnp.where(lane == 0, _score_key(sn_ref[...]), KEY_INVALID)

    thr = _kth_threshold(keys_ref, 1, SB, width, TOPK_MAX)
    kc = keys_ref[0]
    thr_t = jnp.concatenate([thr] * (width // LANE), axis=1)
    bias_ref[...] = jnp.where((kc >= thr_t) & (kc != KEY_INVALID), 0.0, NEG)


def _select_sample_call(page_table, qi3, wbc, sn, cache_kidx, layer):
    n, n_pages = page_table.shape
    page = cache_kidx.shape[2]
    width = n_pages * page + LANE
    kern = functools.partial(_select_sample_kernel, layer=layer, n_pages=n_pages, page=page)
    return pl.pallas_call(
        kern,
        grid_spec=pltpu.PrefetchScalarGridSpec(
            num_scalar_prefetch=1,
            grid=(n // SB,),
            in_specs=[
                pl.BlockSpec((SB, N_IDX_HEADS, LANE), lambda s, pt: (s, 0, 0)),
                pl.BlockSpec((SB, N_IDX_HEADS, LANE), lambda s, pt: (s, 0, 0)),
                pl.BlockSpec((SB, LANE), lambda s, pt: (s, 0)),
                pl.BlockSpec(memory_space=pl.ANY),
            ],
            out_specs=pl.BlockSpec((SB, width), lambda s, pt: (s, 0)),
            scratch_shapes=[
                pltpu.VMEM((2, SB, n_pages * page, IDX_DIM), F32),
                pltpu.VMEM((1, SB, width), I32),
                pltpu.SemaphoreType.DMA((2,)),
            ],
        ),
        out_shape=jax.ShapeDtypeStruct((n, width), F32),
        compiler_params=_params(("arbitrary",)),
        name="dsa_select_sample",
    )(page_table, qi3, wbc, sn, cache_kidx)


def _attend_sample_kernel(pt_ref, q_ref, kn_ref, vn_ref, bias_ref, ck_ref, cv_ref, o_ref,
                          kbuf_ref, vbuf_ref, ksem_ref, vsem_ref, *, layer, n_pages, page):
    b = pl.program_id(0)
    slot = b % 2
    past = n_pages * page

    def copies(sample, sl):
        return (_page_copies(pt_ref, ck_ref, layer, kbuf_ref, ksem_ref, sl, sample, 1, n_pages)
                + _page_copies(pt_ref, cv_ref, layer, vbuf_ref, vsem_ref, sl, sample, 1, n_pages))

    @pl.when(b == 0)
    def _():
        for cp in copies(0, 0):
            cp.start()

    @pl.when(b + 1 < pl.num_programs(0))
    def _():
        for cp in copies(b + 1, 1 - slot):
            cp.start()

    for cp in copies(b, slot):
        cp.wait()

    qb = q_ref[...].astype(BF16)
    qr = qb.astype(F32)
    bias = bias_ref[...]
    head = lax.broadcasted_iota(I32, (N_HEADS, HEAD_DIM), 0)
    o = jnp.zeros((N_HEADS, HEAD_DIM), F32)
    for g in range(N_KV_HEADS):
        rows = pl.ds(g, past, stride=N_KV_HEADS)
        s = _dot_nt(qb, kbuf_ref[slot, 0, rows, :].astype(BF16)) + bias[:, :past]
        k_new = kn_ref[g:g + 1, :].astype(BF16).astype(F32)
        v_new = vn_ref[g:g + 1, :].astype(BF16).astype(F32)
        s_new = jnp.sum(qr * k_new, axis=1, keepdims=True) + bias[:, past:past + 1]
        m = jnp.maximum(jnp.max(s, axis=1, keepdims=True), s_new)
        p = jnp.exp(s - m)
        p_new = jnp.exp(s_new - m)
        denom = jnp.sum(p, axis=1, keepdims=True) + p_new
        og = _dot(p.astype(BF16), vbuf_ref[slot, 0, rows, :].astype(BF16))
        og = (og + p_new.astype(BF16).astype(F32) * v_new) / denom
        o = jnp.where(head // KV_GROUP == g, og, o)
    o_ref[...] = o


def _attend_sample_call(page_table, q3, kn3, vn3, bias3, cache_k, cache_v, layer):
    n, n_pages = page_table.shape
    page = cache_k.shape[2] // N_KV_HEADS
    past = n_pages * page
    kern = functools.partial(_attend_sample_kernel, layer=layer, n_pages=n_pages, page=page)
    per = lambda b, pt: (b, 0, 0)
    return pl.pallas_call(
        kern,
        grid_spec=pltpu.PrefetchScalarGridSpec(
            num_scalar_prefetch=1,
            grid=(n,),
            in_specs=[
                pl.BlockSpec((None, N_HEADS, HEAD_DIM), per),
                pl.BlockSpec((None, N_KV_HEADS, HEAD_DIM), per),
                pl.BlockSpec((None, N_KV_HEADS, HEAD_DIM), per),
                pl.BlockSpec((None, 1, past + LANE), per),
                pl.BlockSpec(memory_space=pl.ANY),
                pl.BlockSpec(memory_space=pl.ANY),
            ],
            out_specs=pl.BlockSpec((None, N_HEADS, HEAD_DIM), per),
            scratch_shapes=[
                pltpu.VMEM((2, 1, N_KV_HEADS * past, HEAD_DIM), F32),
                pltpu.VMEM((2, 1, N_KV_HEADS * past, HEAD_DIM), F32),
                pltpu.SemaphoreType.DMA((2,)),
                pltpu.SemaphoreType.DMA((2,)),
            ],
        ),
        out_shape=jax.ShapeDtypeStruct((n, N_HEADS, HEAD_DIM), F32),
        compiler_params=_params(("arbitrary",)),
        name="dsa_attend_sample",
    )(page_table, q3, kn3, vn3, bias3, cache_k, cache_v)


def _out_sample_kernel(x_ref, mod_ref, lng_ref, lnb_ref, o_ref, w_out_ref, y_ref):
    y = _dot(o_ref[...].astype(BF16), w_out_ref[...])
    y_ref[...] = _post_norm(x_ref[...], y, mod_ref[2], lng_ref[...], lnb_ref[...])


def _out_sample_call(x, mod_s, lng, lnb, o, w_out):
    args = (x, mod_s, _row(lng), _row(lnb), o, w_out)
    return pl.pallas_call(
        _out_sample_kernel,
        grid=(1,),
        in_specs=[_const_spec(a.shape) for a in args],
        out_specs=_full_spec(x.shape),
        out_shape=jax.ShapeDtypeStruct(x.shape, F32),
        compiler_params=_params(("arbitrary",)),
        name="dsa_out_sample",
    )(*args)


def _pack_proj_weight(w_in):
    d = w_in.shape[0]
    o_qi = Q_COLS + 2 * KV_COLS
    o_ki = o_qi + QI_COLS
    o_wi = o_ki + IDX_DIM
    qi = w_in[:, o_qi:o_ki].reshape(d, N_IDX_HEADS, IDX_DIM)
    qi = jnp.pad(qi, ((0, 0), (0, 0), (0, LANE - IDX_DIM))).reshape(d, QI_PAD)
    ki = jnp.pad(w_in[:, o_ki:o_wi], ((0, 0), (0, LANE - IDX_DIM)))
    wi = jnp.pad(w_in[:, o_wi:], ((0, 0), (0, LANE - N_IDX_HEADS)))
    return jnp.concatenate([w_in[:, :o_qi], qi, ki, wi], axis=1).astype(BF16)


def kernel(x_prompt, x_sample, cache_k, cache_v, cache_kidx, state_conv, page_table, c_prompt, c_sample,
           w_ada, b_ada, ln_g, ln_b, sgu_w_in, sgu_b_in, sgu_norm_g, sgu_norm_b, sgu_w_s, sgu_b_s, sgu_w_out,
           dsa_w_in, dsa_w_out, ffn_w_up, ffn_conv_w, ffn_conv_b, ffn_w_down):
    bsz, t_p, _ = x_prompt.shape
    n_s = x_sample.shape[0]
    n_phys, page = cache_k.shape[1], cache_k.shape[2]
    past = page_table.shape[1] * page

    c_prompt8 = jnp.pad(c_prompt, ((0, SUBLANE - bsz), (0, 0)))
    mods_s, mods_p = _ada_call(c_sample, c_prompt8, w_ada, b_ada)

    ck = cache_k.reshape(cache_k.shape[0], n_phys, page * N_KV_HEADS, HEAD_DIM)
    cv = cache_v.reshape(cache_v.shape[0], n_phys, page * N_KV_HEADS, HEAD_DIM)

    xp = x_prompt
    xs = x_sample.reshape(n_s, D_MODEL)
    kp_l, vp_l, kip_l, ks_l, vs_l, kis_l, sgu_l, convp_l, convs_l = [], [], [], [], [], [], [], [], []
    for i in range(DEPTH):
        j = i // 2
        mod_p, mod_s = mods_p[i], mods_s[i]
        if i % 2 == 0:
            w_in = sgu_w_in[j].astype(BF16)
            w_out = sgu_w_out[j].astype(BF16)
            wtril = jnp.tril(sgu_w_s[j]).astype(BF16)
            bs_full = jnp.repeat(sgu_b_s[j].T, SGU_GROUP_DIM, axis=1)
            ws0 = jnp.repeat(sgu_w_s[j][:, 0, 0], SGU_GROUP_DIM)
            bs0 = jnp.repeat(sgu_b_s[j][:, 0], SGU_GROUP_DIM)
            xp = _sgu_prompt_call(xp, mod_p, ln_g[i, 0], ln_b[i, 0], w_in, sgu_b_in[j], sgu_norm_g[j],
                                  sgu_norm_b[j], wtril, bs_full, w_out)
            xs, v_rows = _sgu_sample_call(xs, mod_s, ln_g[i, 0], ln_b[i, 0], w_in, sgu_b_in[j], sgu_norm_g[j],
                                          sgu_norm_b[j], ws0, bs0, w_out)
            sgu_l.append(v_rows.reshape(n_s, 1, D_SGU))
        else:
            w_proj = _pack_proj_weight(dsa_w_in[j])
            w_out = dsa_w_out[j].astype(BF16)
            q, k, v, kb, vb, qi_hm, ki, kib, wi = _proj_prompt_call(xp, mod_p, w_proj)
            xp = _attend_prompt_call(xp, mod_p, ln_g[i, 0], ln_b[i, 0], q, qi_hm, wi, kb, vb, kib, w_out)
            kp_l.append(k.reshape(bsz, t_p, N_KV_HEADS, HEAD_DIM))
            vp_l.append(v.reshape(bsz, t_p, N_KV_HEADS, HEAD_DIM))
            kip_l.append(ki)

            qs, ks_new, vs_new, qis, kis_new, wis, sn = _proj_sample_call(xs, mod_s, w_proj)
            wbc = jnp.broadcast_to(wis[:, :N_IDX_HEADS, None], (n_s, N_IDX_HEADS, LANE))
            bias = _select_sample_call(page_table, qis.reshape(n_s, N_IDX_HEADS, LANE), wbc, sn, cache_kidx, j)
            o = _attend_sample_call(page_table, qs.reshape(n_s, N_HEADS, HEAD_DIM),
                                    ks_new.reshape(n_s, N_KV_HEADS, HEAD_DIM),
                                    vs_new.reshape(n_s, N_KV_HEADS, HEAD_DIM),
                                    bias.reshape(n_s, 1, past + LANE), ck, cv, j)
            xs = _out_sample_call(xs, mod_s, ln_g[i, 0], ln_b[i, 0], o.reshape(n_s, Q_COLS), w_out)
            ks_l.append(ks_new.reshape(n_s, 1, N_KV_HEADS, HEAD_DIM))
            vs_l.append(vs_new.reshape(n_s, 1, N_KV_HEADS, HEAD_DIM))
            kis_l.append(kis_new.reshape(n_s, 1, IDX_DIM))

        w_up = ffn_w_up[i].astype(BF16)
        w_down = ffn_w_down[i].astype(BF16)
        xp, conv_p = _ffn_prompt_call(xp, mod_p, ln_g[i, 1], ln_b[i, 1], w_up, ffn_conv_w[i], ffn_conv_b[i], w_down)
        past_t = jnp.swapaxes(state_conv[i], 0, 1)
        xs, a_s = _ffn_sample_call(xs, mod_s, ln_g[i, 1], ln_b[i, 1], w_up, ffn_conv_w[i], ffn_conv_b[i], w_down,
                                   past_t)
        convp_l.append(conv_p)
        convs_l.append(jnp.stack([state_conv[i][:, 1], a_s], axis=1))

    return (xp, xs.reshape(n_s, 1, D_MODEL),
            jnp.stack(kp_l), jnp.stack(vp_l), jnp.stack(kip_l),
            jnp.stack(ks_l), jnp.stack(vs_l), jnp.stack(kis_l),
            jnp.stack(sgu_l), jnp.stack(convp_l), jnp.stack(convs_l))
```

```python
import functools

import jax
import jax.numpy as jnp
from jax import lax
from jax.experimental import pallas as pl
from jax.experimental.pallas import tpu as pltpu

F32 = jnp.float32
BF16 = jnp.bfloat16
I32 = jnp.int32

D_MODEL = 1024
DEPTH = 4
N_MOD = 6
CHUNK = 128
D_SGU = D_MODEL
SGU_GROUPS = 8
SGU_GROUP_DIM = D_SGU // SGU_GROUPS
N_HEADS = 8
HEAD_DIM = D_MODEL // N_HEADS
N_KV_HEADS = 2
KV_GROUP = N_HEADS // N_KV_HEADS
N_IDX_HEADS = 8
IDX_DIM = 64
TOPK_MAX = 256
Q_COLS = N_HEADS * HEAD_DIM
KV_COLS = N_KV_HEADS * HEAD_DIM
QI_COLS = N_IDX_HEADS * IDX_DIM
D_FF = 2816
CONV_W = 3
ALPHA = (2 * DEPTH) ** 0.25
LN_EPS = 1e-5

LANE = 128
SUBLANE = 8
VMEM_LIMIT = 56 * 1024 * 1024

TM = 512
FF_CHUNK = 256
N_FF_CHUNKS = D_FF // FF_CHUNK
QB = 128
TK = 512
KEY_MIN = -2 ** 31
NEG = -1e30
QI_PAD = N_IDX_HEADS * LANE
PROJ_COLS = Q_COLS + 2 * KV_COLS + QI_PAD + 2 * LANE
SB = 8


def _dot(a, b):
    return jnp.dot(a, b, preferred_element_type=F32)


def _dot_nt(a, b):
    return lax.dot_general(a, b, (((1,), (1,)), ((), ())), preferred_element_type=F32)


def _ln(x):
    mu = jnp.mean(x, axis=-1, keepdims=True)
    xc = x - mu
    var = jnp.mean(xc * xc, axis=-1, keepdims=True)
    return xc * lax.rsqrt(var + LN_EPS)


def _modulate(x, shift, scale):
    return x * (1.0 + scale) + shift


def _post_norm(x, y, gate, g, b):
    return _ln(ALPHA * x + (1.0 + gate) * y) * g + b


def _prompt_mod(mod_ref, m):
    return mod_ref[m, pl.ds(pl.program_id(0), 1), :]


def _key_value(key):
    bits = jnp.where(key < 0, jnp.int32(KEY_MIN) - key, key)
    return lax.bitcast_convert_type(bits, F32)


def _kth_threshold(count_ge, shape, kth):
    def bit_body(it, key):
        cand = key + jnp.left_shift(jnp.int32(1), 31 - it)
        return jnp.where(count_ge(_key_value(cand)) >= kth, cand, key)

    key = lax.fori_loop(0, 32, bit_body, jnp.full(shape, KEY_MIN, I32))
    return jnp.where(key == KEY_MIN, -jnp.inf, _key_value(key))


def _params(sem=None):
    return pltpu.CompilerParams(dimension_semantics=sem, vmem_limit_bytes=VMEM_LIMIT)


def _const_spec(shape):
    return pl.BlockSpec(shape, lambda *_: (0,) * len(shape), pipeline_mode=pl.Buffered(1))


def _full_spec(shape):
    return pl.BlockSpec(shape, lambda *_: (0,) * len(shape))


def _ada_kernel(cs_ref, cp_ref, w_ref, b_ref, os_ref, op_ref):
    w = w_ref[...].astype(BF16)
    bias = b_ref[...]
    os_ref[...] = _dot(jax.nn.silu(cs_ref[...]).astype(BF16), w) + bias
    op_ref[...] = _dot(jax.nn.silu(cp_ref[...]).astype(BF16), w) + bias


def _ada_call(c_sample, c_prompt8, w_ada, b_ada):
    tn = 512
    nn = D_MODEL // tn
    n_s = c_sample.shape[0]
    return pl.pallas_call(
        _ada_kernel,
        grid=(DEPTH, N_MOD, nn),
        in_specs=[
            pl.BlockSpec((n_s, D_MODEL), lambda l, m, n: (0, 0)),
            pl.BlockSpec((SUBLANE, D_MODEL), lambda l, m, n: (0, 0)),
            pl.BlockSpec((None, D_MODEL, tn), lambda l, m, n: (l, 0, m * nn + n)),
            pl.BlockSpec((None, 1, tn), lambda l, m, n: (l, 0, m * nn + n)),
        ],
        out_specs=[
            pl.BlockSpec((None, None, n_s, tn), lambda l, m, n: (l, m, 0, n)),
            pl.BlockSpec((None, None, SUBLANE, tn), lambda l, m, n: (l, m, 0, n)),
        ],
        out_shape=[
            jax.ShapeDtypeStruct((DEPTH, N_MOD, n_s, D_MODEL), F32),
            jax.ShapeDtypeStruct((DEPTH, N_MOD, SUBLANE, D_MODEL), F32),
        ],
        compiler_params=_params(("arbitrary",) * 3),
        name="ada_params",
    )(c_sample, c_prompt8, w_ada, b_ada.reshape(DEPTH, 1, N_MOD * D_MODEL))


def _sgu_front(x, shift, scale, w_in_ref, b_in_ref, ng_ref, nb_ref):
    h = _modulate(x, shift, scale).astype(BF16)
    u = jax.nn.gelu(_dot(h, w_in_ref[:, :D_SGU]) + b_in_ref[:, :D_SGU])
    v = jax.nn.gelu(_dot(h, w_in_ref[:, D_SGU:]) + b_in_ref[:, D_SGU:])
    v = _ln(v) * ng_ref[...] + nb_ref[...]
    return u, v


def _sgu_prompt_kernel(x_ref, mod_ref, lng_ref, lnb_ref, w_in_ref, b_in_ref, ng_ref, nb_ref,
                       wtril_ref, bs_ref, w_out_ref, o_ref, gated_ref):
    x = x_ref[...]
    u, v = _sgu_front(x, _prompt_mod(mod_ref, 0), _prompt_mod(mod_ref, 1),
                      w_in_ref, b_in_ref, ng_ref, nb_ref)
    vb = v.astype(BF16)
    n_chunks = x.shape[0] // CHUNK
    for g in range(SGU_GROUPS):
        cols = slice(g * SGU_GROUP_DIM, (g + 1) * SGU_GROUP_DIM)
        rhs = jnp.concatenate([vb[n * CHUNK:(n + 1) * CHUNK, cols] for n in range(n_chunks)], axis=1)
        mixed = _dot(wtril_ref[g], rhs)
        for n in range(n_chunks):
            rows = slice(n * CHUNK, (n + 1) * CHUNK)
            mix_n = mixed[:, n * SGU_GROUP_DIM:(n + 1) * SGU_GROUP_DIM] + bs_ref[:, cols]
            gated_ref[rows, cols] = (u[rows, cols] * mix_n).astype(BF16)
    y = _dot(gated_ref[...], w_out_ref[...])
    o_ref[...] = _post_norm(x, y, _prompt_mod(mod_ref, 2), lng_ref[...], lnb_ref[...])


def _sgu_sample_kernel(x_ref, mod_ref, lng_ref, lnb_ref, w_in_ref, b_in_ref, ng_ref, nb_ref,
                       ws0_ref, bs0_ref, w_out_ref, o_ref, v_ref):
    x = x_ref[...]
    u, v = _sgu_front(x, mod_ref[0], mod_ref[1], w_in_ref, b_in_ref, ng_ref, nb_ref)
    v_ref[...] = v
    mixed = v * ws0_ref[...] + bs0_ref[...]
    y = _dot((u * mixed).astype(BF16), w_out_ref[...])
    o_ref[...] = _post_norm(x, y, mod_ref[2], lng_ref[...], lnb_ref[...])


def _row(v):
    return v.reshape(1, -1)


def _sgu_prompt_call(x, mod_p, lng, lnb, w_in, b_in, ng, nb, wtril, bs_full, w_out):
    bsz, t, _ = x.shape
    return pl.pallas_call(
        _sgu_prompt_kernel,
        grid=(bsz, t // TM),
        in_specs=[
            pl.BlockSpec((None, TM, D_MODEL), lambda b, i: (b, i, 0)),
            _const_spec(mod_p.shape),
            _const_spec((1, D_MODEL)), _const_spec((1, D_MODEL)),
            _const_spec(w_in.shape), _const_spec((1, 2 * D_SGU)),
            _const_spec((1, D_SGU)), _const_spec((1, D_SGU)),
            _const_spec(wtril.shape), _const_spec(bs_full.shape), _const_spec(w_out.shape),
        ],
        out_specs=pl.BlockSpec((None, TM, D_MODEL), lambda b, i: (b, i, 0)),
        out_shape=jax.ShapeDtypeStruct(x.shape, F32),
        scratch_shapes=[pltpu.VMEM((TM, D_SGU), BF16)],
        compiler_params=_params(("arbitrary", "arbitrary")),
        name="sgu_prompt",
    )(x, mod_p, _row(lng), _row(lnb), w_in, _row(b_in), _row(ng), _row(nb), wtril, bs_full, w_out)


def _sgu_sample_call(x, mod_s, lng, lnb, w_in, b_in, ng, nb, ws0, bs0, w_out):
    n = x.shape[0]
    args = (x, mod_s, _row(lng), _row(lnb), w_in, _row(b_in), _row(ng), _row(nb), _row(ws0), _row(bs0), w_out)
    return pl.pallas_call(
        _sgu_sample_kernel,
        grid=(1,),
        in_specs=[_const_spec(a.shape) for a in args],
        out_specs=[_full_spec((n, D_MODEL)), _full_spec((n, D_SGU))],
        out_shape=[jax.ShapeDtypeStruct((n, D_MODEL), F32), jax.ShapeDtypeStruct((n, D_SGU), F32)],
        compiler_params=_params(("arbitrary",)),
        name="sgu_sample",
    )(*args)


def _ffn_prompt_kernel(x_ref, mod_ref, lng_ref, lnb_ref, w_up_ref, cw_ref, cb_ref, w_down_ref,
                       o_ref, st_ref, abuf_ref, g_ref):
    i = pl.program_id(1)
    tm = x_ref.shape[0]

    @pl.when(i == 0)
    def _():
        abuf_ref[:, 0:SUBLANE, :] = jnp.zeros((N_FF_CHUNKS, SUBLANE, FF_CHUNK), F32)

    x = x_ref[...]
    h = _modulate(x, _prompt_mod(mod_ref, 3), _prompt_mod(mod_ref, 4)).astype(BF16)
    for c in range(N_FF_CHUNKS):
        cols = slice(c * FF_CHUNK, (c + 1) * FF_CHUNK)
        ucols = slice(D_FF + c * FF_CHUNK, D_FF + (c + 1) * FF_CHUNK)
        a = _dot(h, w_up_ref[:, cols])
        abuf_ref[c, SUBLANE:tm + SUBLANE, :] = a
        a_m1 = abuf_ref[c, SUBLANE - 1:tm + SUBLANE - 1, :]
        a_m2 = abuf_ref[c, SUBLANE - 2:tm + SUBLANE - 2, :]
        conv = a_m2 * cw_ref[0:1, cols] + a_m1 * cw_ref[1:2, cols] + a * cw_ref[2:3, cols] + cb_ref[:, cols]
        abuf_ref[c, 0:SUBLANE, :] = abuf_ref[c, tm:tm + SUBLANE, :]
        u = _dot(h, w_up_ref[:, ucols])
        g_ref[:, cols] = (jax.nn.gelu(conv) * u).astype(BF16)

    @pl.when(i == pl.num_programs(1) - 1)
    def _():
        for c in range(N_FF_CHUNKS):
            st_ref[:, c * FF_CHUNK:(c + 1) * FF_CHUNK] = abuf_ref[c, SUBLANE - (CONV_W - 1):SUBLANE, :]

    y = _dot(g_ref[...], w_down_ref[...])
    o_ref[...] = _post_norm(x, y, _prompt_mod(mod_ref, 5), lng_ref[...], lnb_ref[...])


def _ffn_sample_kernel(x_ref, mod_ref, lng_ref, lnb_ref, w_up_ref, cw_ref, cb_ref, w_down_ref, past_ref,
                       o_ref, a_ref, g_ref):
    x = x_ref[...]
    h = _modulate(x, mod_ref[3], mod_ref[4]).astype(BF16)
    for c in range(N_FF_CHUNKS):
        cols = slice(c * FF_CHUNK, (c + 1) * FF_CHUNK)
        ucols = slice(D_FF + c * FF_CHUNK, D_FF + (c + 1) * FF_CHUNK)
        a = _dot(h, w_up_ref[:, cols])
        a_ref[:, cols] = a
        conv = (past_ref[0, :, cols] * cw_ref[0:1, cols] + past_ref[1, :, cols] * cw_ref[1:2, cols]
                + a * cw_ref[2:3, cols] + cb_ref[:, cols])
        u = _dot(h, w_up_ref[:, ucols])
        g_ref[:, cols] = (jax.nn.gelu(conv) * u).astype(BF16)
    y = _dot(g_ref[...], w_down_ref[...])
    o_ref[...] = _post_norm(x, y, mod_ref[5], lng_ref[...], lnb_ref[...])


def _ffn_prompt_call(x, mod_p, lng, lnb, w_up, cw, cb, w_down):
    bsz, t, _ = x.shape
    return pl.pallas_call(
        _ffn_prompt_kernel,
        grid=(bsz, t // TM),
        in_specs=[
            pl.BlockSpec((None, TM, D_MODEL), lambda b, i: (b, i, 0)),
            _const_spec(mod_p.shape),
            _const_spec((1, D_MODEL)), _const_spec((1, D_MODEL)),
            _const_spec(w_up.shape), _const_spec(cw.shape), _const_spec((1, D_FF)), _const_spec(w_down.shape),
        ],
        out_specs=[
            pl.BlockSpec((None, TM, D_MODEL), lambda b, i: (b, i, 0)),
            pl.BlockSpec((None, CONV_W - 1, D_FF), lambda b, i: (b, 0, 0)),
        ],
        out_shape=[jax.ShapeDtypeStruct(x.shape, F32), jax.ShapeDtypeStruct((bsz, CONV_W - 1, D_FF), F32)],
        scratch_shapes=[pltpu.VMEM((N_FF_CHUNKS, TM + SUBLANE, FF_CHUNK), F32), pltpu.VMEM((TM, D_FF), BF16)],
        compiler_params=_params(("arbitrary", "arbitrary")),
        name="ffn_prompt",
    )(x, mod_p, _row(lng), _row(lnb), w_up, cw, _row(cb), w_down)


def _ffn_sample_call(x, mod_s, lng, lnb, w_up, cw, cb, w_down, past_t):
    n = x.shape[0]
    args = (x, mod_s, _row(lng), _row(lnb), w_up, cw, _row(cb), w_down, past_t)
    return pl.pallas_call(
        _ffn_sample_kernel,
        grid=(1,),
        in_specs=[_const_spec(a.shape) for a in args],
        out_specs=[_full_spec((n, D_MODEL)), _full_spec((n, D_FF))],
        out_shape=[jax.ShapeDtypeStruct((n, D_MODEL), F32), jax.ShapeDtypeStruct((n, D_FF), F32)],
        scratch_shapes=[pltpu.VMEM((n, D_FF), BF16)],
        compiler_params=_params(("arbitrary",)),
        name="ffn_sample",
    )(*args)


_C_K = Q_COLS
_C_V = _C_K + KV_COLS
_C_QI = _C_V + KV_COLS
_C_KI = _C_QI + QI_PAD
_C_WI = _C_KI + LANE


def _proj_common(h, w_ref):
    q = _dot(h, w_ref[:, 0:_C_K]) * HEAD_DIM ** -0.5
    k = _dot(h, w_ref[:, _C_K:_C_V])
    v = _dot(h, w_ref[:, _C_V:_C_QI])
    qi = _dot(h, w_ref[:, _C_QI:_C_KI])
    ki = _dot(h, w_ref[:, _C_KI:_C_WI])
    wi = _dot(h, w_ref[:, _C_WI:PROJ_COLS]) * N_IDX_HEADS ** -0.5 * IDX_DIM ** -0.5
    return q, k, v, qi, ki, wi


def _store_kv_rows(ref, x):
    n = x.shape[0]
    for g in range(N_KV_HEADS):
        ref[pl.ds(g, n, stride=N_KV_HEADS), :] = x[:, g * HEAD_DIM:(g + 1) * HEAD_DIM]


def _proj_prompt_kernel(x_ref, mod_ref, w_ref, wt_ref, q_ref, k_ref, v_ref, kb_ref, vt_ref, qi_ref, ki_ref, kib_ref,
                        wit_ref):
    h = _modulate(x_ref[...], _prompt_mod(mod_ref, 0), _prompt_mod(mod_ref, 1)).astype(BF16)
    q, k, v, qi, ki, _ = _proj_common(h, w_ref)
    q_ref[...] = q.astype(BF16)
    _store_kv_rows(k_ref, k)
    _store_kv_rows(v_ref, v)
    kb_ref[...] = k.astype(BF16)
    for hh in range(N_IDX_HEADS):
        qi_ref[hh] = qi[:, hh * LANE:(hh + 1) * LANE].astype(BF16)
    ki_ref[...] = ki[:, :IDX_DIM]
    kib_ref[...] = ki.astype(BF16)
    vt_ref[0] = _dot_nt(wt_ref[0:KV_COLS, :], h).astype(BF16)
    wit = _dot_nt(wt_ref[KV_COLS:, :], h) * N_IDX_HEADS ** -0.5 * IDX_DIM ** -0.5
    wit_ref[...] = wit[0:N_IDX_HEADS, :]


def _proj_sample_kernel(x_ref, mod_ref, w_ref, q_ref, k_ref, v_ref, qi_ref, ki_ref, wi_ref, sn_ref):
    h = _modulate(x_ref[...], mod_ref[0], mod_ref[1]).astype(BF16)
    q, k, v, qi, ki, wi = _proj_common(h, w_ref)
    q_ref[...] = q
    _store_kv_rows(k_ref, k)
    _store_kv_rows(v_ref, v)
    qi_ref[...] = qi
    ki_ref[...] = ki[:, :IDX_DIM]
    wi_ref[...] = wi
    kr = ki.astype(BF16).astype(F32)
    sn = jnp.zeros((x_ref.shape[0], 1), F32)
    for hh in range(N_IDX_HEADS):
        qr = qi[:, hh * LANE:(hh + 1) * LANE].astype(BF16).astype(F32)
        sh = jnp.sum(qr * kr, axis=1, keepdims=True)
        sn = sn + jnp.maximum(sh, 0.0) * wi[:, hh:hh + 1]
    sn_ref[...] = jnp.broadcast_to(sn, sn_ref.shape)


def _proj_prompt_call(x, mod_p, w_proj, w_proj_t):
    bsz, t, _ = x.shape
    assert TM == TK
    tok = lambda b, i: (b, i, 0)
    return pl.pallas_call(
        _proj_prompt_kernel,
        grid=(bsz, t // TM),
        in_specs=[pl.BlockSpec((None, TM, D_MODEL), tok), _const_spec(mod_p.shape), _const_spec(w_proj.shape),
                  _const_spec(w_proj_t.shape)],
        out_specs=[
            pl.BlockSpec((None, TM, Q_COLS), tok),
            pl.BlockSpec((None, N_KV_HEADS * TM, HEAD_DIM), tok), pl.BlockSpec((None, N_KV_HEADS * TM, HEAD_DIM), tok),
            pl.BlockSpec((None, TM, KV_COLS), tok),
            pl.BlockSpec((None, 1, KV_COLS, TK), lambda b, i: (b, i, 0, 0)),
            pl.BlockSpec((None, N_IDX_HEADS, TM, LANE), lambda b, i: (b, 0, i, 0)),
            pl.BlockSpec((None, TM, IDX_DIM), tok),
            pl.BlockSpec((None, TM, LANE), tok),
            pl.BlockSpec((None, N_IDX_HEADS, TM), lambda b, i: (b, 0, i)),
        ],
        out_shape=[
            jax.ShapeDtypeStruct((bsz, t, Q_COLS), BF16),
            jax.ShapeDtypeStruct((bsz, N_KV_HEADS * t, HEAD_DIM), F32),
            jax.ShapeDtypeStruct((bsz, N_KV_HEADS * t, HEAD_DIM), F32),
            jax.ShapeDtypeStruct((bsz, t, KV_COLS), BF16),
            jax.ShapeDtypeStruct((bsz, t // TK, KV_COLS, TK), BF16),
            jax.ShapeDtypeStruct((bsz, N_IDX_HEADS, t, LANE), BF16),
            jax.ShapeDtypeStruct((bsz, t, IDX_DIM), F32),
            jax.ShapeDtypeStruct((bsz, t, LANE), BF16),
            jax.ShapeDtypeStruct((bsz, N_IDX_HEADS, t), F32),
        ],
        compiler_params=_params(("arbitrary", "arbitrary")),
        name="dsa_proj_prompt",
    )(x, mod_p, w_proj, w_proj_t)


def _proj_sample_call(x, mod_s, w_proj):
    n = x.shape[0]
    args = (x, mod_s, w_proj)
    kv_shape = (N_KV_HEADS * n, HEAD_DIM)
    shapes = ((n, Q_COLS), kv_shape, kv_shape, (n, QI_PAD), (n, IDX_DIM), (n, LANE), (n, LANE))
    return pl.pallas_call(
        _proj_sample_kernel,
        grid=(1,),
        in_specs=[_const_spec(a.shape) for a in args],
        out_specs=[_full_spec(s) for s in shapes],
        out_shape=[jax.ShapeDtypeStruct(s, F32) for s in shapes],
        compiler_params=_params(("arbitrary",)),
        name="dsa_proj_sample",
    )(*args)


def _attend_prompt_kernel(x_ref, mod_ref, lng_ref, lnb_ref, q_ref, qi_ref, wit_ref, kb_ref, vt_ref, kib_ref,
                          w_out_ref, o_ref, sc_ref, q4_ref, acc_ref):
    i = pl.program_id(1)
    n_chunks = (i * QB) // TK + 1
    sub_tiles = TK // SUBLANE

    for g in range(N_KV_HEADS):
        for hh in range(KV_GROUP):
            head = g * KV_GROUP + hh
            q4_ref[g, hh * QB:(hh + 1) * QB, :] = q_ref[:, head * HEAD_DIM:(head + 1) * HEAD_DIM]

    k_pos = lax.broadcasted_iota(I32, (TK, QB), 0)
    q_pos = i * QB + lax.broadcasted_iota(I32, (TK, QB), 1)

    def score_body(c, carry):
        start = pl.multiple_of(c * TK, TK)
        kslab = kib_ref[pl.ds(start, TK), :]
        sidx = jnp.zeros((TK, QB), F32)
        for pair in range(N_IDX_HEADS // 2):
            s = _dot_nt(kslab, qi_ref[2 * pair:2 * pair + 2].reshape(2 * QB, LANE))
            for j in range(2):
                hh = 2 * pair + j
                sidx = sidx + jnp.maximum(s[:, j * QB:(j + 1) * QB], 0.0) * wit_ref[hh:hh + 1, :]
        sc_ref[c] = jnp.where(k_pos + start <= q_pos, sidx, jnp.nan)
        return carry

    lax.fori_loop(0, n_chunks, score_body, 0)

    def count_ge(cand):
        def body(c, accs):
            kc = sc_ref[c]
            accs = list(accs)
            for t in range(sub_tiles):
                hit = jnp.where(kc[t * SUBLANE:(t + 1) * SUBLANE, :] >= cand, 1.0, 0.0)
                accs[t % len(accs)] = accs[t % len(accs)] + hit
            return tuple(accs)

        zero = jnp.zeros((SUBLANE, QB), F32)
        a0, a1, a2, a3 = lax.fori_loop(0, n_chunks, body, (zero,) * 4)
        return jnp.sum((a0 + a1) + (a2 + a3), axis=0, keepdims=True)

    thr = _kth_threshold(count_ge, (SUBLANE, QB), TOPK_MAX)
    thr_full = jnp.concatenate([thr] * sub_tiles, axis=0)

    acc_ref[...] = jnp.zeros(acc_ref.shape, F32)

    def att_body(c, carry):
        m, l = carry
        start = pl.multiple_of(c * TK, TK)
        bias = jnp.where(sc_ref[c] >= thr_full, 0.0, NEG)
        bias4 = jnp.concatenate([bias] * KV_GROUP, axis=1)
        m_out, l_out = [], []
        for g in range(N_KV_HEADS):
            cols = slice(g * HEAD_DIM, (g + 1) * HEAD_DIM)
            s = _dot_nt(kb_ref[pl.ds(start, TK), cols], q4_ref[g]) + bias4
            m_new = jnp.maximum(m[g], jnp.max(s, axis=0, keepdims=True))
            alpha = jnp.exp(m[g] - m_new)
            p = jnp.exp(s - m_new)
            l_out.append(alpha * l[g] + jnp.sum(p, axis=0, keepdims=True))
            acc_ref[g] = alpha * acc_ref[g] + _dot(vt_ref[c, cols, :], p.astype(BF16))
            m_out.append(m_new)
        return tuple(m_out), tuple(l_out)

    m0 = (jnp.full((1, KV_GROUP * QB), NEG, F32),) * N_KV_HEADS
    l0 = (jnp.zeros((1, KV_GROUP * QB), F32),) * N_KV_HEADS
    _, l = lax.fori_loop(0, n_chunks, att_body, (m0, l0))

    heads = []
    for g in range(N_KV_HEADS):
        og = acc_ref[g] / l[g]
        heads += [og[:, hh * QB:(hh + 1) * QB].T for hh in range(KV_GROUP)]
    o = jnp.concatenate(heads, axis=1).astype(BF16)
    y = _dot(o, w_out_ref[...])
    o_ref[...] = _post_norm(x_ref[...], y, _prompt_mod(mod_ref, 2), lng_ref[...], lnb_ref[...])


def _attend_prompt_call(x, mod_p, lng, lnb, q, qi_hm, wit, kb, vt, kib, w_out):
    bsz, t, _ = x.shape
    tok = lambda b, i: (b, i, 0)
    seq = lambda b, i: (b, 0, 0)
    return pl.pallas_call(
        _attend_prompt_kernel,
        grid=(bsz, t // QB),
        in_specs=[
            pl.BlockSpec((None, QB, D_MODEL), tok),
            _const_spec(mod_p.shape), _const_spec((1, D_MODEL)), _const_spec((1, D_MODEL)),
            pl.BlockSpec((None, QB, Q_COLS), tok),
            pl.BlockSpec((None, N_IDX_HEADS, QB, LANE), lambda b, i: (b, 0, i, 0)),
            pl.BlockSpec((None, N_IDX_HEADS, QB), lambda b, i: (b, 0, i)),
            pl.BlockSpec((None, t, KV_COLS), seq),
            pl.BlockSpec((None, t // TK, KV_COLS, TK), lambda b, i: (b, 0, 0, 0)),
            pl.BlockSpec((None, t, LANE), seq),
            _const_spec(w_out.shape),
        ],
        out_specs=pl.BlockSpec((None, QB, D_MODEL), tok),
        out_shape=jax.ShapeDtypeStruct(x.shape, F32),
        scratch_shapes=[
            pltpu.VMEM((t // TK, TK, QB), F32),
            pltpu.VMEM((N_KV_HEADS, KV_GROUP * QB, HEAD_DIM), BF16),
            pltpu.VMEM((N_KV_HEADS, HEAD_DIM, KV_GROUP * QB), F32),
        ],
        compiler_params=_params(("arbitrary", "arbitrary")),
        name="dsa_attend_prompt",
    )(x, mod_p, _row(lng), _row(lnb), q, qi_hm, wit, kb, vt, kib, w_out)


def _page_copies(pt_ref, cache_ref, layer, buf_ref, sem_ref, slot, first_sample, n_samples, n_pages, along_lanes=False):
    rows, cols = cache_ref.shape[2:]
    copies = []
    for bb in range(n_samples):
        for p in range(n_pages):
            src = cache_ref.at[layer, pt_ref[first_sample + bb, p]]
            if along_lanes:
                dst = buf_ref.at[slot, bb, :, pl.ds(p * cols, cols)]
            else:
                dst = buf_ref.at[slot, bb, pl.ds(p * rows, rows), :]
            copies.append(pltpu.make_async_copy(src, dst, sem_ref.at[slot]))
    return copies


def _select_sample_kernel(pt_ref, qi_ref, wbc_ref, sn_ref, cache_ref, bias_ref, buf_ref, sc_ref, sem_ref,
                          *, layer, n_pages, page):
    step = pl.program_id(0)
    slot = step % 2
    past = n_pages * page
    width = past + LANE

    def copies(s, sl):
        return _page_copies(pt_ref, cache_ref, layer, buf_ref, sem_ref, sl, s * SB, SB, n_pages, along_lanes=True)

    @pl.when(step == 0)
    def _():
        for cp in copies(0, 0):
            cp.start()

    @pl.when(step + 1 < pl.num_programs(0))
    def _():
        for cp in copies(step + 1, 1 - slot):
            cp.start()

    for cp in copies(step, slot):
        cp.wait()

    for bb in range(SB):
        qs = qi_ref[bb][:, :IDX_DIM].astype(BF16)
        s = _dot(qs, buf_ref[slot, bb].astype(BF16))
        w = jnp.concatenate([wbc_ref[bb]] * (past // LANE), axis=1)
        sidx = jnp.sum(jnp.maximum(s, 0.0) * w, axis=0, keepdims=True)
        sc_ref[bb:bb + 1, 0:past] = sidx
    lane = lax.broadcasted_iota(I32, (SB, LANE), 1)
    sc_ref[:, past:width] = jnp.where(lane == 0, sn_ref[...], jnp.nan)

    lane_tiles = width // LANE
    sc = sc_ref[...]

    def count_ge(cand):
        acc = jnp.zeros((SB, LANE), F32)
        for t in range(lane_tiles):
            acc = acc + jnp.where(sc[:, t * LANE:(t + 1) * LANE] >= cand, 1.0, 0.0)
        return jnp.sum(acc, axis=1, keepdims=True)

    thr = _kth_threshold(count_ge, (SB, LANE), TOPK_MAX)
    bias_ref[...] = jnp.where(sc >= jnp.concatenate([thr] * lane_tiles, axis=1), 0.0, NEG)


def _select_sample_call(page_table, qi3, wbc, sn, cache_kidx_t, layer):
    n, n_pages = page_table.shape
    page = cache_kidx_t.shape[3]
    width = n_pages * page + LANE
    kern = functools.partial(_select_sample_kernel, layer=layer, n_pages=n_pages, page=page)
    return pl.pallas_call(
        kern,
        grid_spec=pltpu.PrefetchScalarGridSpec(
            num_scalar_prefetch=1,
            grid=(n // SB,),
            in_specs=[
                pl.BlockSpec((SB, N_IDX_HEADS, LANE), lambda s, pt: (s, 0, 0)),
                pl.BlockSpec((SB, N_IDX_HEADS, LANE), lambda s, pt: (s, 0, 0)),
                pl.BlockSpec((SB, LANE), lambda s, pt: (s, 0)),
                pl.BlockSpec(memory_space=pl.ANY),
            ],
            out_specs=pl.BlockSpec((SB, width), lambda s, pt: (s, 0)),
            scratch_shapes=[
                pltpu.VMEM((2, SB, IDX_DIM, n_pages * page), F32),
                pltpu.VMEM((SB, width), F32),
                pltpu.SemaphoreType.DMA((2,)),
            ],
        ),
        out_shape=jax.ShapeDtypeStruct((n, width), F32),
        compiler_params=_params(("arbitrary",)),
        name="dsa_select_sample",
    )(page_table, qi3, wbc, sn, cache_kidx_t)


def _attend_sample_kernel(pt_ref, q_ref, kn_ref, vn_ref, bias_ref, ck_ref, cv_ref, o_ref,
                          kbuf_ref, vbuf_ref, ksem_ref, vsem_ref, *, layer, n_pages, page):
    b = pl.program_id(0)
    slot = b % 2
    past = n_pages * page

    def copies(sample, sl):
        return (_page_copies(pt_ref, ck_ref, layer, kbuf_ref, ksem_ref, sl, sample, 1, n_pages)
                + _page_copies(pt_ref, cv_ref, layer, vbuf_ref, vsem_ref, sl, sample, 1, n_pages))

    @pl.when(b == 0)
    def _():
        for cp in copies(0, 0):
            cp.start()

    @pl.when(b + 1 < pl.num_programs(0))
    def _():
        for cp in copies(b + 1, 1 - slot):
            cp.start()

    for cp in copies(b, slot):
        cp.wait()

    qb = q_ref[...].astype(BF16)
    qr = qb.astype(F32)
    bias = bias_ref[...]
    head = lax.broadcasted_iota(I32, (N_HEADS, HEAD_DIM), 0)
    o = jnp.zeros((N_HEADS, HEAD_DIM), F32)
    for g in range(N_KV_HEADS):
        rows = pl.ds(g, past, stride=N_KV_HEADS)
        s = _dot_nt(qb, kbuf_ref[slot, 0, rows, :].astype(BF16)) + bias[:, :past]
        k_new = kn_ref[g:g + 1, :].astype(BF16).astype(F32)
        v_new = vn_ref[g:g + 1, :].astype(BF16).astype(F32)
        s_new = jnp.sum(qr * k_new, axis=1, keepdims=True) + bias[:, past:past + 1]
        m = jnp.maximum(jnp.max(s, axis=1, keepdims=True), s_new)
        p = jnp.exp(s - m)
        p_new = jnp.exp(s_new - m)
        denom = jnp.sum(p, axis=1, keepdims=True) + p_new
        og = _dot(p.astype(BF16), vbuf_ref[slot, 0, rows, :].astype(BF16))
        og = (og + p_new.astype(BF16).astype(F32) * v_new) / denom
        o = jnp.where(head // KV_GROUP == g, og, o)
    o_ref[...] = o


def _attend_sample_call(page_table, q3, kn3, vn3, bias3, cache_k, cache_v, layer):
    n, n_pages = page_table.shape
    page = cache_k.shape[2] // N_KV_HEADS
    past = n_pages * page
    kern = functools.partial(_attend_sample_kernel, layer=layer, n_pages=n_pages, page=page)
    per = lambda b, pt: (b, 0, 0)
    return pl.pallas_call(
        kern,
        grid_spec=pltpu.PrefetchScalarGridSpec(
            num_scalar_prefetch=1,
            grid=(n,),
            in_specs=[
                pl.BlockSpec((None, N_HEADS, HEAD_DIM), per),
                pl.BlockSpec((None, N_KV_HEADS, HEAD_DIM), per),
                pl.BlockSpec((None, N_KV_HEADS, HEAD_DIM), per),
                pl.BlockSpec((None, 1, past + LANE), per),
                pl.BlockSpec(memory_space=pl.ANY),
                pl.BlockSpec(memory_space=pl.ANY),
            ],
            out_specs=pl.BlockSpec((None, N_HEADS, HEAD_DIM), per),
            scratch_shapes=[
                pltpu.VMEM((2, 1, N_KV_HEADS * past, HEAD_DIM), F32),
                pltpu.VMEM((2, 1, N_KV_HEADS * past, HEAD_DIM), F32),
                pltpu.SemaphoreType.DMA((2,)),
                pltpu.SemaphoreType.DMA((2,)),
            ],
        ),
        out_shape=jax.ShapeDtypeStruct((n, N_HEADS, HEAD_DIM), F32),
        compiler_params=_params(("arbitrary",)),
        name="dsa_attend_sample",
    )(page_table, q3, kn3, vn3, bias3, cache_k, cache_v)


def _out_sample_kernel(x_ref, mod_ref, lng_ref, lnb_ref, o_ref, w_out_ref, y_ref):
    y = _dot(o_ref[...].astype(BF16), w_out_ref[...])
    y_ref[...] = _post_norm(x_ref[...], y, mod_ref[2], lng_ref[...], lnb_ref[...])


def _out_sample_call(x, mod_s, lng, lnb, o, w_out):
    args = (x, mod_s, _row(lng), _row(lnb), o, w_out)
    return pl.pallas_call(
        _out_sample_kernel,
        grid=(1,),
        in_specs=[_const_spec(a.shape) for a in args],
        out_specs=_full_spec(x.shape),
        out_shape=jax.ShapeDtypeStruct(x.shape, F32),
        compiler_params=_params(("arbitrary",)),
        name="dsa_out_sample",
    )(*args)


def _pack_proj_weight(w_in):
    d = w_in.shape[0]
    o_qi = Q_COLS + 2 * KV_COLS
    o_ki = o_qi + QI_COLS
    o_wi = o_ki + IDX_DIM
    qi = w_in[:, o_qi:o_ki].reshape(d, N_IDX_HEADS, IDX_DIM)
    qi = jnp.pad(qi, ((0, 0), (0, 0), (0, LANE - IDX_DIM))).reshape(d, QI_PAD)
    ki = jnp.pad(w_in[:, o_ki:o_wi], ((0, 0), (0, LANE - IDX_DIM)))
    wi = jnp.pad(w_in[:, o_wi:], ((0, 0), (0, LANE - N_IDX_HEADS)))
    return jnp.concatenate([w_in[:, :o_qi], qi, ki, wi], axis=1).astype(BF16)


def _pack_proj_weight_t(w_in):
    o_v = Q_COLS + KV_COLS
    o_wi = Q_COLS + 2 * KV_COLS + QI_COLS + IDX_DIM
    wt = jnp.concatenate([w_in[:, o_v:o_v + KV_COLS], w_in[:, o_wi:]], axis=1).T
    return jnp.pad(wt, ((0, 2 * SUBLANE - N_IDX_HEADS), (0, 0))).astype(BF16)


def kernel(x_prompt, x_sample, cache_k, cache_v, cache_kidx, state_conv, page_table, c_prompt, c_sample,
           w_ada, b_ada, ln_g, ln_b, sgu_w_in, sgu_b_in, sgu_norm_g, sgu_norm_b, sgu_w_s, sgu_b_s, sgu_w_out,
           dsa_w_in, dsa_w_out, ffn_w_up, ffn_conv_w, ffn_conv_b, ffn_w_down):
    bsz, t_p, _ = x_prompt.shape
    n_s = x_sample.shape[0]
    n_phys, page = cache_k.shape[1], cache_k.shape[2]
    past = page_table.shape[1] * page

    c_prompt8 = jnp.pad(c_prompt, ((0, SUBLANE - bsz), (0, 0)))
    mods_s, mods_p = _ada_call(c_sample, c_prompt8, w_ada, b_ada)

    ck = cache_k.reshape(cache_k.shape[0], n_phys, page * N_KV_HEADS, HEAD_DIM)
    cv = cache_v.reshape(cache_v.shape[0], n_phys, page * N_KV_HEADS, HEAD_DIM)
    ckidx_t = jnp.swapaxes(cache_kidx, 2, 3)

    xp = x_prompt
    xs = x_sample.reshape(n_s, D_MODEL)
    kp_l, vp_l, kip_l, ks_l, vs_l, kis_l, sgu_l, convp_l, convs_l = [], [], [], [], [], [], [], [], []
    for i in range(DEPTH):
        j = i // 2
        mod_p, mod_s = mods_p[i], mods_s[i]
        if i % 2 == 0:
            w_in = sgu_w_in[j].astype(BF16)
            w_out = sgu_w_out[j].astype(BF16)
            wtril = jnp.tril(sgu_w_s[j]).astype(BF16)
            bs_full = jnp.repeat(sgu_b_s[j].T, SGU_GROUP_DIM, axis=1)
            ws0 = jnp.repeat(sgu_w_s[j][:, 0, 0], SGU_GROUP_DIM)
            bs0 = jnp.repeat(sgu_b_s[j][:, 0], SGU_GROUP_DIM)
            xp = _sgu_prompt_call(xp, mod_p, ln_g[i, 0], ln_b[i, 0], w_in, sgu_b_in[j], sgu_norm_g[j],
                                  sgu_norm_b[j], wtril, bs_full, w_out)
            xs, v_rows = _sgu_sample_call(xs, mod_s, ln_g[i, 0], ln_b[i, 0], w_in, sgu_b_in[j], sgu_norm_g[j],
                                          sgu_norm_b[j], ws0, bs0, w_out)
            sgu_l.append(v_rows.reshape(n_s, 1, D_SGU))
        else:
            w_proj = _pack_proj_weight(dsa_w_in[j])
            w_out = dsa_w_out[j].astype(BF16)
            q, k, v, kb, vt, qi_hm, ki, kib, wit = _proj_prompt_call(xp, mod_p, w_proj, _pack_proj_weight_t(dsa_w_in[j]))
            xp = _attend_prompt_call(xp, mod_p, ln_g[i, 0], ln_b[i, 0], q, qi_hm, wit, kb, vt, kib, w_out)
            kp_l.append(k.reshape(bsz, t_p, N_KV_HEADS, HEAD_DIM))
            vp_l.append(v.reshape(bsz, t_p, N_KV_HEADS, HEAD_DIM))
            kip_l.append(ki)

            qs, ks_new, vs_new, qis, kis_new, wis, sn = _proj_sample_call(xs, mod_s, w_proj)
            wbc = jnp.broadcast_to(wis[:, :N_IDX_HEADS, None], (n_s, N_IDX_HEADS, LANE))
            bias = _select_sample_call(page_table, qis.reshape(n_s, N_IDX_HEADS, LANE), wbc, sn, ckidx_t, j)
            o = _attend_sample_call(page_table, qs.reshape(n_s, N_HEADS, HEAD_DIM),
                                    ks_new.reshape(n_s, N_KV_HEADS, HEAD_DIM),
                                    vs_new.reshape(n_s, N_KV_HEADS, HEAD_DIM),
                                    bias.reshape(n_s, 1, past + LANE), ck, cv, j)
            xs = _out_sample_call(xs, mod_s, ln_g[i, 0], ln_b[i, 0], o.reshape(n_s, Q_COLS), w_out)
            ks_l.append(ks_new.reshape(n_s, 1, N_KV_HEADS, HEAD_DIM))
            vs_l.append(vs_new.reshape(n_s, 1, N_KV_HEADS, HEAD_DIM))
            kis_l.append(kis_new.reshape(n_s, 1, IDX_DIM))

        w_up = ffn_w_up[i].astype(BF16)
        w_down = ffn_w_down[i].astype(BF16)
        xp, conv_p = _ffn_prompt_call(xp, mod_p, ln_g[i, 1], ln_b[i, 1], w_up, ffn_conv_w[i], ffn_conv_b[i], w_down)
        past_t = jnp.swapaxes(state_conv[i], 0, 1)
        xs, a_s = _ffn_sample_call(xs, mod_s, ln_g[i, 1], ln_b[i, 1], w_up, ffn_conv_w[i], ffn_conv_b[i], w_down,
                                   past_t)
        convp_l.append(conv_p)
        convs_l.append(jnp.stack([state_conv[i][:, 1], a_s], axis=1))

    return (xp, xs.reshape(n_s, 1, D_MODEL),
            jnp.stack(kp_l), jnp.stack(vp_l), jnp.stack(kip_l),
            jnp.stack(ks_l), jnp.stack(vs_l), jnp.stack(kis_l),
            jnp.stack(sgu_l), jnp.stack(convp_l), jnp.stack(convs_l))
```

```python
import functools

import jax
import jax.numpy as jnp
from jax import lax
from jax.experimental import pallas as pl
from jax.experimental.pallas import tpu as pltpu

F32 = jnp.float32
BF16 = jnp.bfloat16
I32 = jnp.int32

D_MODEL = 1024
DEPTH = 4
N_MOD = 6
CHUNK = 128
D_SGU = D_MODEL
SGU_GROUPS = 8
SGU_GROUP_DIM = D_SGU // SGU_GROUPS
N_HEADS = 8
HEAD_DIM = D_MODEL // N_HEADS
N_KV_HEADS = 2
KV_GROUP = N_HEADS // N_KV_HEADS
N_IDX_HEADS = 8
IDX_DIM = 64
TOPK_MAX = 256
Q_COLS = N_HEADS * HEAD_DIM
KV_COLS = N_KV_HEADS * HEAD_DIM
QI_COLS = N_IDX_HEADS * IDX_DIM
D_FF = 2816
CONV_W = 3
ALPHA = (2 * DEPTH) ** 0.25
LN_EPS = 1e-5

LANE = 128
SUBLANE = 8
VMEM_LIMIT = 56 * 1024 * 1024

TM = 512
FF_CHUNK = 256
N_FF_CHUNKS = D_FF // FF_CHUNK
QB = 128
TK = 512
KEY_MIN = -2 ** 31
NEG = -1e30
QI_PAD = N_IDX_HEADS * LANE
PROJ_COLS = Q_COLS + 2 * KV_COLS + QI_PAD + 2 * LANE
SB = 8
LOG2E = 1.4426950408889634
VT_ROWS = HEAD_DIM + 16


def _dot(a, b):
    return jnp.dot(a, b, preferred_element_type=F32)


def _dot_nt(a, b):
    return lax.dot_general(a, b, (((1,), (1,)), ((), ())), preferred_element_type=F32)


def _ln(x):
    mu = jnp.mean(x, axis=-1, keepdims=True)
    xc = x - mu
    var = jnp.mean(xc * xc, axis=-1, keepdims=True)
    return xc * lax.rsqrt(var + LN_EPS)


def _modulate(x, shift, scale):
    return x * (1.0 + scale) + shift


def _post_norm(x, y, gate, g, b):
    return _ln(ALPHA * x + (1.0 + gate) * y) * g + b


def _prompt_mod(mod_ref, m):
    return mod_ref[m, pl.ds(pl.program_id(0), 1), :]


def _key_value(key):
    bits = jnp.where(key < 0, jnp.int32(KEY_MIN) - key, key)
    return lax.bitcast_convert_type(bits, F32)


def _kth_threshold(count_ge, shape, kth):
    def bit_body(it, key):
        cand = key + jnp.left_shift(jnp.int32(1), 31 - it)
        return jnp.where(count_ge(_key_value(cand)) >= kth, cand, key)

    key = lax.fori_loop(0, 32, bit_body, jnp.full(shape, KEY_MIN, I32))
    return jnp.where(key == KEY_MIN, -jnp.inf, _key_value(key))


def _params(sem=None):
    return pltpu.CompilerParams(dimension_semantics=sem, vmem_limit_bytes=VMEM_LIMIT)


def _const_spec(shape):
    return pl.BlockSpec(shape, lambda *_: (0,) * len(shape), pipeline_mode=pl.Buffered(1))


def _full_spec(shape):
    return pl.BlockSpec(shape, lambda *_: (0,) * len(shape))


def _resident_spec(a):
    if isinstance(a, tuple):
        arr, layer = a
        tail = (0,) * (arr.ndim - 1)
        return pl.BlockSpec((None,) + arr.shape[1:], lambda *_: (layer,) + tail, pipeline_mode=pl.Buffered(1))
    return _const_spec(a.shape)


def _arrays(args):
    return [a[0] if isinstance(a, tuple) else a for a in args]


def _ada_kernel(cs_ref, cp_ref, w_ref, b_ref, os_ref, op_ref):
    w = w_ref[...].astype(BF16)
    bias = b_ref[...]
    os_ref[...] = _dot(jax.nn.silu(cs_ref[...]).astype(BF16), w) + bias
    op_ref[...] = _dot(jax.nn.silu(cp_ref[...]).astype(BF16), w) + bias


def _ada_call(c_sample, c_prompt8, w_ada, b_ada):
    tn = 512
    nn = D_MODEL // tn
    n_s = c_sample.shape[0]
    return pl.pallas_call(
        _ada_kernel,
        grid=(DEPTH, N_MOD, nn),
        in_specs=[
            pl.BlockSpec((n_s, D_MODEL), lambda l, m, n: (0, 0)),
            pl.BlockSpec((SUBLANE, D_MODEL), lambda l, m, n: (0, 0)),
            pl.BlockSpec((None, D_MODEL, tn), lambda l, m, n: (l, 0, m * nn + n)),
            pl.BlockSpec((None, 1, tn), lambda l, m, n: (l, 0, m * nn + n)),
        ],
        out_specs=[
            pl.BlockSpec((None, None, n_s, tn), lambda l, m, n: (l, m, 0, n)),
            pl.BlockSpec((None, None, SUBLANE, tn), lambda l, m, n: (l, m, 0, n)),
        ],
        out_shape=[
            jax.ShapeDtypeStruct((DEPTH, N_MOD, n_s, D_MODEL), F32),
            jax.ShapeDtypeStruct((DEPTH, N_MOD, SUBLANE, D_MODEL), F32),
        ],
        compiler_params=_params(("arbitrary",) * 3),
        name="ada_params",
    )(c_sample, c_prompt8, w_ada, b_ada.reshape(DEPTH, 1, N_MOD * D_MODEL))


def _sgu_front(x, shift, scale, w_in_ref, b_in_ref, ng_ref, nb_ref):
    h = _modulate(x, shift, scale).astype(BF16)
    u = jax.nn.gelu(_dot(h, w_in_ref[:, :D_SGU]) + b_in_ref[:, :D_SGU])
    v = jax.nn.gelu(_dot(h, w_in_ref[:, D_SGU:]) + b_in_ref[:, D_SGU:])
    v = _ln(v) * ng_ref[...] + nb_ref[...]
    return u, v


def _sgu_prompt_kernel(x_ref, mod_ref, lng_ref, lnb_ref, w_in_ref, b_in_ref, ng_ref, nb_ref,
                       wtril_ref, bs_ref, w_out_ref, o_ref, gated_ref):
    x = x_ref[...]
    u, v = _sgu_front(x, _prompt_mod(mod_ref, 0), _prompt_mod(mod_ref, 1),
                      w_in_ref, b_in_ref, ng_ref, nb_ref)
    vb = v.astype(BF16)
    n_chunks = x.shape[0] // CHUNK
    for g in range(SGU_GROUPS):
        cols = slice(g * SGU_GROUP_DIM, (g + 1) * SGU_GROUP_DIM)
        rhs = jnp.concatenate([vb[n * CHUNK:(n + 1) * CHUNK, cols] for n in range(n_chunks)], axis=1)
        mixed = _dot(wtril_ref[g], rhs)
        for n in range(n_chunks):
            rows = slice(n * CHUNK, (n + 1) * CHUNK)
            mix_n = mixed[:, n * SGU_GROUP_DIM:(n + 1) * SGU_GROUP_DIM] + bs_ref[:, cols]
            gated_ref[rows, cols] = (u[rows, cols] * mix_n).astype(BF16)
    y = _dot(gated_ref[...], w_out_ref[...])
    o_ref[...] = _post_norm(x, y, _prompt_mod(mod_ref, 2), lng_ref[...], lnb_ref[...])


def _sgu_sample_kernel(x_ref, mod_ref, lng_ref, lnb_ref, w_in_ref, b_in_ref, ng_ref, nb_ref,
                       ws0_ref, bs0_ref, w_out_ref, o_ref, v_ref):
    x = x_ref[...]
    u, v = _sgu_front(x, mod_ref[0], mod_ref[1], w_in_ref, b_in_ref, ng_ref, nb_ref)
    v_ref[...] = v
    mixed = v * ws0_ref[...] + bs0_ref[...]
    y = _dot((u * mixed).astype(BF16), w_out_ref[...])
    o_ref[...] = _post_norm(x, y, mod_ref[2], lng_ref[...], lnb_ref[...])


def _row(v):
    return v.reshape(1, -1)


def _sgu_prompt_call(x, mod_p, lng, lnb, w_in, b_in, ng, nb, wtril, bs_full, w_out):
    bsz, t, _ = x.shape
    resident = (mod_p, _row(lng), _row(lnb), w_in, _row(b_in), _row(ng), _row(nb), wtril, bs_full, w_out)
    return pl.pallas_call(
        _sgu_prompt_kernel,
        grid=(bsz, t // TM),
        in_specs=[pl.BlockSpec((None, TM, D_MODEL), lambda b, i: (b, i, 0))] + [_resident_spec(a) for a in resident],
        out_specs=pl.BlockSpec((None, TM, D_MODEL), lambda b, i: (b, i, 0)),
        out_shape=jax.ShapeDtypeStruct(x.shape, F32),
        scratch_shapes=[pltpu.VMEM((TM, D_SGU), BF16)],
        compiler_params=_params(("arbitrary", "arbitrary")),
        name="sgu_prompt",
    )(x, *_arrays(resident))


def _sgu_sample_call(x, mod_s, lng, lnb, w_in, b_in, ng, nb, ws0, bs0, w_out):
    n = x.shape[0]
    args = (x, mod_s, _row(lng), _row(lnb), w_in, _row(b_in), _row(ng), _row(nb), _row(ws0), _row(bs0), w_out)
    return pl.pallas_call(
        _sgu_sample_kernel,
        grid=(1,),
        in_specs=[_resident_spec(a) for a in args],
        out_specs=[_full_spec((n, D_MODEL)), _full_spec((n, D_SGU))],
        out_shape=[jax.ShapeDtypeStruct((n, D_MODEL), F32), jax.ShapeDtypeStruct((n, D_SGU), F32)],
        compiler_params=_params(("arbitrary",)),
        name="sgu_sample",
    )(*_arrays(args))


def _ffn_prompt_kernel(x_ref, mod_ref, lng_ref, lnb_ref, w_up_ref, cw_ref, cb_ref, w_down_ref,
                       o_ref, st_ref, abuf_ref, g_ref, h_ref):
    i = pl.program_id(1)
    tm = x_ref.shape[0]

    @pl.when(i == 0)
    def _():
        abuf_ref[:, 0:SUBLANE, :] = jnp.zeros((N_FF_CHUNKS, SUBLANE, FF_CHUNK), F32)

    h_ref[...] = _modulate(x_ref[...], _prompt_mod(mod_ref, 3), _prompt_mod(mod_ref, 4)).astype(BF16)

    def up(c):
        a = _dot(h_ref[...], w_up_ref[:, c * FF_CHUNK:(c + 1) * FF_CHUNK])
        u = _dot(h_ref[...], w_up_ref[:, D_FF + c * FF_CHUNK:D_FF + (c + 1) * FF_CHUNK])
        return a, u

    nxt = up(0)
    for c in range(N_FF_CHUNKS):
        cols = slice(c * FF_CHUNK, (c + 1) * FF_CHUNK)
        a, u = nxt
        abuf_ref[c, SUBLANE:tm + SUBLANE, :] = a
        if c + 1 < N_FF_CHUNKS:
            nxt = up(c + 1)
        a_m1 = abuf_ref[c, SUBLANE - 1:tm + SUBLANE - 1, :]
        a_m2 = abuf_ref[c, SUBLANE - 2:tm + SUBLANE - 2, :]
        conv = a_m2 * cw_ref[0:1, cols] + a_m1 * cw_ref[1:2, cols] + a * cw_ref[2:3, cols] + cb_ref[:, cols]
        abuf_ref[c, 0:SUBLANE, :] = abuf_ref[c, tm:tm + SUBLANE, :]
        g_ref[:, cols] = (jax.nn.gelu(conv) * u).astype(BF16)

    @pl.when(i == pl.num_programs(1) - 1)
    def _():
        for c in range(N_FF_CHUNKS):
            st_ref[:, c * FF_CHUNK:(c + 1) * FF_CHUNK] = abuf_ref[c, SUBLANE - (CONV_W - 1):SUBLANE, :]

    gate, lng, lnb = _prompt_mod(mod_ref, 5), lng_ref[...], lnb_ref[...]
    half = tm // 2
    for r in range(2):
        rows = slice(r * half, (r + 1) * half)
        y = _dot(g_ref[rows, :], w_down_ref[...])
        o_ref[rows, :] = _post_norm(x_ref[rows, :], y, gate, lng, lnb)


def _ffn_sample_kernel(x_ref, mod_ref, lng_ref, lnb_ref, w_up_ref, cw_ref, cb_ref, w_down_ref, past_ref,
                       o_ref, a_ref, g_ref):
    x = x_ref[...]
    h = _modulate(x, mod_ref[3], mod_ref[4]).astype(BF16)
    for c in range(N_FF_CHUNKS):
        cols = slice(c * FF_CHUNK, (c + 1) * FF_CHUNK)
        ucols = slice(D_FF + c * FF_CHUNK, D_FF + (c + 1) * FF_CHUNK)
        a = _dot(h, w_up_ref[:, cols])
        a_ref[:, cols] = a
        conv = (past_ref[0, :, cols] * cw_ref[0:1, cols] + past_ref[1, :, cols] * cw_ref[1:2, cols]
                + a * cw_ref[2:3, cols] + cb_ref[:, cols])
        u = _dot(h, w_up_ref[:, ucols])
        g_ref[:, cols] = (jax.nn.gelu(conv) * u).astype(BF16)
    y = _dot(g_ref[...], w_down_ref[...])
    o_ref[...] = _post_norm(x, y, mod_ref[5], lng_ref[...], lnb_ref[...])


def _ffn_prompt_call(x, mod_p, lng, lnb, w_up, cw, cb, w_down):
    bsz, t, _ = x.shape
    resident = (mod_p, _row(lng), _row(lnb), w_up, cw, _row(cb), w_down)
    return pl.pallas_call(
        _ffn_prompt_kernel,
        grid=(bsz, t // TM),
        in_specs=[pl.BlockSpec((None, TM, D_MODEL), lambda b, i: (b, i, 0))] + [_resident_spec(a) for a in resident],
        out_specs=[
            pl.BlockSpec((None, TM, D_MODEL), lambda b, i: (b, i, 0)),
            pl.BlockSpec((None, CONV_W - 1, D_FF), lambda b, i: (b, 0, 0)),
        ],
        out_shape=[jax.ShapeDtypeStruct(x.shape, F32), jax.ShapeDtypeStruct((bsz, CONV_W - 1, D_FF), F32)],
        scratch_shapes=[pltpu.VMEM((N_FF_CHUNKS, TM + SUBLANE, FF_CHUNK), F32), pltpu.VMEM((TM, D_FF), BF16),
                        pltpu.VMEM((TM, D_MODEL), BF16)],
        compiler_params=_params(("arbitrary", "arbitrary")),
        name="ffn_prompt",
    )(x, *_arrays(resident))


def _ffn_sample_call(x, mod_s, lng, lnb, w_up, cw, cb, w_down, past_t):
    n = x.shape[0]
    args = (x, mod_s, _row(lng), _row(lnb), w_up, cw, _row(cb), w_down, past_t)
    return pl.pallas_call(
        _ffn_sample_kernel,
        grid=(1,),
        in_specs=[_resident_spec(a) for a in args],
        out_specs=[_full_spec((n, D_MODEL)), _full_spec((n, D_FF))],
        out_shape=[jax.ShapeDtypeStruct((n, D_MODEL), F32), jax.ShapeDtypeStruct((n, D_FF), F32)],
        scratch_shapes=[pltpu.VMEM((n, D_FF), BF16)],
        compiler_params=_params(("arbitrary",)),
        name="ffn_sample",
    )(*_arrays(args))


_C_K = Q_COLS
_C_V = _C_K + KV_COLS
_C_QI = _C_V + KV_COLS
_C_KI = _C_QI + QI_PAD
_C_WI = _C_KI + LANE


def _proj_common(h, w_ref):
    q = _dot(h, w_ref[:, 0:_C_K]) * HEAD_DIM ** -0.5
    k = _dot(h, w_ref[:, _C_K:_C_V])
    v = _dot(h, w_ref[:, _C_V:_C_QI])
    qi = _dot(h, w_ref[:, _C_QI:_C_KI])
    ki = _dot(h, w_ref[:, _C_KI:_C_WI])
    wi = _dot(h, w_ref[:, _C_WI:PROJ_COLS]) * N_IDX_HEADS ** -0.5 * IDX_DIM ** -0.5
    return q, k, v, qi, ki, wi


def _store_kv_rows(ref, x):
    n = x.shape[0]
    for g in range(N_KV_HEADS):
        ref[pl.ds(g, n, stride=N_KV_HEADS), :] = x[:, g * HEAD_DIM:(g + 1) * HEAD_DIM]


def _proj_prompt_kernel(x_ref, mod_ref, w_ref, wt_ref, *refs, n_carried):
    q_ref, k_ref, v_ref, kb_ref, vt_ref, qi_ref, ki_ref, kib_ref, wit_ref = refs[n_carried:]
    h = _modulate(x_ref[...], _prompt_mod(mod_ref, 0), _prompt_mod(mod_ref, 1)).astype(BF16)
    q, k, v, qi, ki, _ = _proj_common(h, w_ref)
    q_ref[...] = (q * LOG2E).astype(BF16)
    _store_kv_rows(k_ref, k)
    _store_kv_rows(v_ref, v)
    kb_ref[...] = k.astype(BF16)
    for hh in range(N_IDX_HEADS):
        qi_ref[hh] = qi[:, hh * LANE:(hh + 1) * LANE].astype(BF16)
    ki_ref[...] = ki[:, :IDX_DIM]
    kib_ref[...] = ki.astype(BF16)
    vt = _dot_nt(wt_ref[0:KV_COLS, :], h).astype(BF16)
    ones_rows = (lax.broadcasted_iota(I32, (VT_ROWS - HEAD_DIM, vt.shape[1]), 0) == 0).astype(BF16)
    for g in range(N_KV_HEADS):
        vt_ref[0, g * VT_ROWS:g * VT_ROWS + HEAD_DIM, :] = vt[g * HEAD_DIM:(g + 1) * HEAD_DIM, :]
        vt_ref[0, g * VT_ROWS + HEAD_DIM:(g + 1) * VT_ROWS, :] = ones_rows
    wit = _dot_nt(wt_ref[KV_COLS:, :], h) * N_IDX_HEADS ** -0.5 * IDX_DIM ** -0.5
    wit_ref[...] = wit[0:N_IDX_HEADS, :]


def _proj_sample_kernel(x_ref, mod_ref, w_ref, q_ref, k_ref, v_ref, qi_ref, ki_ref, wi_ref, sn_ref):
    h = _modulate(x_ref[...], mod_ref[0], mod_ref[1]).astype(BF16)
    q, k, v, qi, ki, wi = _proj_common(h, w_ref)
    q_ref[...] = q
    _store_kv_rows(k_ref, k)
    _store_kv_rows(v_ref, v)
    qi_ref[...] = qi
    ki_ref[...] = ki[:, :IDX_DIM]
    wi_ref[...] = wi
    kr = ki.astype(BF16).astype(F32)
    sn = jnp.zeros((x_ref.shape[0], 1), F32)
    for hh in range(N_IDX_HEADS):
        qr = qi[:, hh * LANE:(hh + 1) * LANE].astype(BF16).astype(F32)
        sh = jnp.sum(qr * kr, axis=1, keepdims=True)
        sn = sn + jnp.maximum(sh, 0.0) * wi[:, hh:hh + 1]
    sn_ref[...] = jnp.broadcast_to(sn, sn_ref.shape)


def _proj_prompt_call(x, mod_p, w_proj, w_proj_t, slot, n_slots, kv_stacks):
    bsz, t, _ = x.shape
    assert TM == TK
    tok = lambda b, i: (b, i, 0)
    resident = (mod_p, w_proj, w_proj_t)
    carried = tuple(kv_stacks)
    kv_spec = pl.BlockSpec((None, None, N_KV_HEADS * TM, HEAD_DIM), lambda b, i: (slot, b, i, 0))
    kv_shape = jax.ShapeDtypeStruct((n_slots, bsz, N_KV_HEADS * t, HEAD_DIM), F32)
    return pl.pallas_call(
        functools.partial(_proj_prompt_kernel, n_carried=len(carried)),
        grid=(bsz, t // TM),
        in_specs=([pl.BlockSpec((None, TM, D_MODEL), tok)] + [_resident_spec(a) for a in resident]
                  + [pl.BlockSpec(memory_space=pl.ANY)] * len(carried)),
        input_output_aliases={1 + len(resident) + n: 1 + n for n in range(len(carried))},
        out_specs=[
            pl.BlockSpec((None, TM, Q_COLS), tok),
            kv_spec, kv_spec,
            pl.BlockSpec((None, TM, KV_COLS), tok),
            pl.BlockSpec((None, 1, N_KV_HEADS * VT_ROWS, TK), lambda b, i: (b, i, 0, 0)),
            pl.BlockSpec((None, N_IDX_HEADS, TM, LANE), lambda b, i: (b, 0, i, 0)),
            pl.BlockSpec((None, TM, IDX_DIM), tok),
            pl.BlockSpec((None, TM, LANE), tok),
            pl.BlockSpec((None, N_IDX_HEADS, TM), lambda b, i: (b, 0, i)),
        ],
        out_shape=[
            jax.ShapeDtypeStruct((bsz, t, Q_COLS), BF16),
            kv_shape, kv_shape,
            jax.ShapeDtypeStruct((bsz, t, KV_COLS), BF16),
            jax.ShapeDtypeStruct((bsz, t // TK, N_KV_HEADS * VT_ROWS, TK), BF16),
            jax.ShapeDtypeStruct((bsz, N_IDX_HEADS, t, LANE), BF16),
            jax.ShapeDtypeStruct((bsz, t, IDX_DIM), F32),
            jax.ShapeDtypeStruct((bsz, t, LANE), BF16),
            jax.ShapeDtypeStruct((bsz, N_IDX_HEADS, t), F32),
        ],
        compiler_params=_params(("arbitrary", "arbitrary")),
        name="dsa_proj_prompt",
    )(x, *_arrays(resident), *carried)


def _proj_sample_call(x, mod_s, w_proj):
    n = x.shape[0]
    args = (x, mod_s, w_proj)
    kv_shape = (N_KV_HEADS * n, HEAD_DIM)
    shapes = ((n, Q_COLS), kv_shape, kv_shape, (n, QI_PAD), (n, IDX_DIM), (n, LANE), (n, LANE))
    return pl.pallas_call(
        _proj_sample_kernel,
        grid=(1,),
        in_specs=[_resident_spec(a) for a in args],
        out_specs=[_full_spec(s) for s in shapes],
        out_shape=[jax.ShapeDtypeStruct(s, F32) for s in shapes],
        compiler_params=_params(("arbitrary",)),
        name="dsa_proj_sample",
    )(*_arrays(args))


def _attend_prompt_kernel(x_ref, mod_ref, lng_ref, lnb_ref, q_ref, qi_ref, wit_ref, kb_ref, vt_ref, kib_ref,
                          w_out_ref, o_ref, sc_ref, q4_ref, acc_ref):
    i = pl.program_id(1)
    n_chunks = (i * QB) // TK + 1
    sub_tiles = TK // SUBLANE

    for g in range(N_KV_HEADS):
        for hh in range(KV_GROUP):
            head = g * KV_GROUP + hh
            q4_ref[g, hh * QB:(hh + 1) * QB, :] = q_ref[:, head * HEAD_DIM:(head + 1) * HEAD_DIM]

    k_pos = lax.broadcasted_iota(I32, (TK, QB), 0)
    q_pos = i * QB + lax.broadcasted_iota(I32, (TK, QB), 1)

    def score_body(c, carry):
        start = pl.multiple_of(c * TK, TK)
        kslab = kib_ref[pl.ds(start, TK), :]
        sidx = jnp.zeros((TK, QB), F32)
        for pair in range(N_IDX_HEADS // 2):
            s = _dot_nt(kslab, qi_ref[2 * pair:2 * pair + 2].reshape(2 * QB, LANE))
            for j in range(2):
                hh = 2 * pair + j
                sidx = sidx + jnp.maximum(s[:, j * QB:(j + 1) * QB], 0.0) * wit_ref[hh:hh + 1, :]
        sc_ref[c] = jnp.where(k_pos + start <= q_pos, sidx, jnp.nan)
        return carry

    lax.fori_loop(0, n_chunks, score_body, 0)

    def count_ge(cand):
        def body(c, accs):
            kc = sc_ref[c]
            accs = list(accs)
            for t in range(sub_tiles):
                hit = jnp.where(kc[t * SUBLANE:(t + 1) * SUBLANE, :] >= cand, 1.0, 0.0)
                accs[t % len(accs)] = accs[t % len(accs)] + hit
            return tuple(accs)

        zero = jnp.zeros((SUBLANE, QB), F32)
        a0, a1, a2, a3 = lax.fori_loop(0, n_chunks, body, (zero,) * 4)
        return jnp.sum((a0 + a1) + (a2 + a3), axis=0, keepdims=True)

    thr = _kth_threshold(count_ge, (SUBLANE, QB), TOPK_MAX)
    thr_full = jnp.concatenate([thr] * sub_tiles, axis=0)

    acc_ref[...] = jnp.zeros(acc_ref.shape, F32)

    def attend_chunks(chunks, m):
        s = []
        for c in chunks:
            start = pl.multiple_of(c * TK, TK)
            bias = jnp.where(sc_ref[c] >= thr_full, 0.0, NEG)
            bias4 = jnp.concatenate([bias] * KV_GROUP, axis=1)
            s.append([_dot_nt(kb_ref[pl.ds(start, TK), g * HEAD_DIM:(g + 1) * HEAD_DIM], q4_ref[g]) + bias4
                      for g in range(N_KV_HEADS)])
        m = list(m)
        for c, s_c in zip(chunks, s):
            for g in range(N_KV_HEADS):
                m_new = jnp.maximum(m[g], jnp.max(s_c[g], axis=0, keepdims=True))
                p = jnp.exp2(s_c[g] - m_new).astype(BF16)
                pv = _dot(vt_ref[c, g * VT_ROWS:(g + 1) * VT_ROWS, :], p)
                acc_ref[g] = jnp.exp2(m[g] - m_new) * acc_ref[g] + pv
                m[g] = m_new
        return tuple(m)

    m = (jnp.full((1, KV_GROUP * QB), NEG, F32),) * N_KV_HEADS
    m = lax.fori_loop(0, n_chunks // 2, lambda j, m: attend_chunks((2 * j, 2 * j + 1), m), m)
    lax.fori_loop(0, n_chunks % 2, lambda _, m: attend_chunks((n_chunks - 1,), m), m)

    heads = []
    for g in range(N_KV_HEADS):
        og = acc_ref[g, 0:HEAD_DIM, :] / acc_ref[g, HEAD_DIM:HEAD_DIM + 1, :]
        heads += [og[:, hh * QB:(hh + 1) * QB].T for hh in range(KV_GROUP)]
    o = jnp.concatenate(heads, axis=1).astype(BF16)
    y = _dot(o, w_out_ref[...])
    o_ref[...] = _post_norm(x_ref[...], y, _prompt_mod(mod_ref, 2), lng_ref[...], lnb_ref[...])


def _attend_prompt_call(x, mod_p, lng, lnb, q, qi_hm, wit, kb, vt, kib, w_out):
    bsz, t, _ = x.shape
    tok = lambda b, i: (b, i, 0)
    seq = lambda b, i: (b, 0, 0)
    return pl.pallas_call(
        _attend_prompt_kernel,
        grid=(bsz, t // QB),
        in_specs=[
            pl.BlockSpec((None, QB, D_MODEL), tok),
            _resident_spec(mod_p), _const_spec((1, D_MODEL)), _const_spec((1, D_MODEL)),
            pl.BlockSpec((None, QB, Q_COLS), tok),
            pl.BlockSpec((None, N_IDX_HEADS, QB, LANE), lambda b, i: (b, 0, i, 0)),
            pl.BlockSpec((None, N_IDX_HEADS, QB), lambda b, i: (b, 0, i)),
            pl.BlockSpec((None, t, KV_COLS), seq),
            pl.BlockSpec((None, t // TK, N_KV_HEADS * VT_ROWS, TK), lambda b, i: (b, 0, 0, 0)),
            pl.BlockSpec((None, t, LANE), seq),
            _resident_spec(w_out),
        ],
        out_specs=pl.BlockSpec((None, QB, D_MODEL), tok),
        out_shape=jax.ShapeDtypeStruct(x.shape, F32),
        scratch_shapes=[
            pltpu.VMEM((t // TK, TK, QB), F32),
            pltpu.VMEM((N_KV_HEADS, KV_GROUP * QB, HEAD_DIM), BF16),
            pltpu.VMEM((N_KV_HEADS, VT_ROWS, KV_GROUP * QB), F32),
        ],
        compiler_params=_params(("arbitrary", "arbitrary")),
        name="dsa_attend_prompt",
    )(x, *_arrays((mod_p, _row(lng), _row(lnb), q, qi_hm, wit, kb, vt, kib, w_out)))


def _page_copies(pt_ref, cache_ref, layer, buf_ref, sem_ref, slot, first_sample, n_samples, n_pages, along_lanes=False):
    rows, cols = cache_ref.shape[2:]
    copies = []
    for bb in range(n_samples):
        for p in range(n_pages):
            src = cache_ref.at[layer, pt_ref[first_sample + bb, p]]
            if along_lanes:
                dst = buf_ref.at[slot, bb, :, pl.ds(p * cols, cols)]
            else:
                dst = buf_ref.at[slot, bb, pl.ds(p * rows, rows), :]
            copies.append(pltpu.make_async_copy(src, dst, sem_ref.at[slot]))
    return copies


def _select_sample_kernel(pt_ref, qi_ref, wbc_ref, sn_ref, cache_ref, bias_ref, buf_ref, sc_ref, sem_ref,
                          *, layer, n_pages, page):
    step = pl.program_id(0)
    slot = step % 2
    past = n_pages * page
    width = past + LANE

    def copies(s, sl):
        return _page_copies(pt_ref, cache_ref, layer, buf_ref, sem_ref, sl, s * SB, SB, n_pages, along_lanes=True)

    @pl.when(step == 0)
    def _():
        for cp in copies(0, 0):
            cp.start()

    @pl.when(step + 1 < pl.num_programs(0))
    def _():
        for cp in copies(step + 1, 1 - slot):
            cp.start()

    for cp in copies(step, slot):
        cp.wait()

    for bb in range(SB):
        qs = qi_ref[bb][:, :IDX_DIM].astype(BF16)
        s = _dot(qs, buf_ref[slot, bb].astype(BF16))
        w = jnp.concatenate([wbc_ref[bb]] * (past // LANE), axis=1)
        sidx = jnp.sum(jnp.maximum(s, 0.0) * w, axis=0, keepdims=True)
        sc_ref[bb:bb + 1, 0:past] = sidx
    lane = lax.broadcasted_iota(I32, (SB, LANE), 1)
    sc_ref[:, past:width] = jnp.where(lane == 0, sn_ref[...], jnp.nan)

    lane_tiles = width // LANE
    sc = sc_ref[...]

    def count_ge(cand):
        acc = jnp.zeros((SB, LANE), F32)
        for t in range(lane_tiles):
            acc = acc + jnp.where(sc[:, t * LANE:(t + 1) * LANE] >= cand, 1.0, 0.0)
        return jnp.sum(acc, axis=1, keepdims=True)

    thr = _kth_threshold(count_ge, (SB, LANE), TOPK_MAX)
    bias_ref[...] = jnp.where(sc >= jnp.concatenate([thr] * lane_tiles, axis=1), 0.0, NEG)


def _select_sample_call(page_table, qi3, wbc, sn, cache_kidx_t, layer):
    n, n_pages = page_table.shape
    page = cache_kidx_t.shape[3]
    width = n_pages * page + LANE
    kern = functools.partial(_select_sample_kernel, layer=layer, n_pages=n_pages, page=page)
    return pl.pallas_call(
        kern,
        grid_spec=pltpu.PrefetchScalarGridSpec(
            num_scalar_prefetch=1,
            grid=(n // SB,),
            in_specs=[
                pl.BlockSpec((SB, N_IDX_HEADS, LANE), lambda s, pt: (s, 0, 0)),
                pl.BlockSpec((SB, N_IDX_HEADS, LANE), lambda s, pt: (s, 0, 0)),
                pl.BlockSpec((SB, LANE), lambda s, pt: (s, 0)),
                pl.BlockSpec(memory_space=pl.ANY),
            ],
            out_specs=pl.BlockSpec((SB, width), lambda s, pt: (s, 0)),
            scratch_shapes=[
                pltpu.VMEM((2, SB, IDX_DIM, n_pages * page), F32),
                pltpu.VMEM((SB, width), F32),
                pltpu.SemaphoreType.DMA((2,)),
            ],
        ),
        out_shape=jax.ShapeDtypeStruct((n, width), F32),
        compiler_params=_params(("arbitrary",)),
        name="dsa_select_sample",
    )(page_table, qi3, wbc, sn, cache_kidx_t)


def _attend_sample_kernel(pt_ref, q_ref, kn_ref, vn_ref, bias_ref, ck_ref, cv_ref, o_ref,
                          kbuf_ref, vbuf_ref, ksem_ref, vsem_ref, *, layer, n_pages, page):
    b = pl.program_id(0)
    slot = b % 2
    past = n_pages * page

    def copies(sample, sl):
        return (_page_copies(pt_ref, ck_ref, layer, kbuf_ref, ksem_ref, sl, sample, 1, n_pages)
                + _page_copies(pt_ref, cv_ref, layer, vbuf_ref, vsem_ref, sl, sample, 1, n_pages))

    @pl.when(b == 0)
    def _():
        for cp in copies(0, 0):
            cp.start()

    @pl.when(b + 1 < pl.num_programs(0))
    def _():
        for cp in copies(b + 1, 1 - slot):
            cp.start()

    for cp in copies(b, slot):
        cp.wait()

    qb = q_ref[...].astype(BF16)
    qr = qb.astype(F32)
    bias = bias_ref[...]
    head = lax.broadcasted_iota(I32, (N_HEADS, HEAD_DIM), 0)
    o = jnp.zeros((N_HEADS, HEAD_DIM), F32)
    for g in range(N_KV_HEADS):
        rows = pl.ds(g, past, stride=N_KV_HEADS)
        s = _dot_nt(qb, kbuf_ref[slot, 0, rows, :].astype(BF16)) + bias[:, :past]
        k_new = kn_ref[g:g + 1, :].astype(BF16).astype(F32)
        v_new = vn_ref[g:g + 1, :].astype(BF16).astype(F32)
        s_new = jnp.sum(qr * k_new, axis=1, keepdims=True) + bias[:, past:past + 1]
        m = jnp.maximum(jnp.max(s, axis=1, keepdims=True), s_new)
        p = jnp.exp(s - m)
        p_new = jnp.exp(s_new - m)
        denom = jnp.sum(p, axis=1, keepdims=True) + p_new
        og = _dot(p.astype(BF16), vbuf_ref[slot, 0, rows, :].astype(BF16))
        og = (og + p_new.astype(BF16).astype(F32) * v_new) / denom
        o = jnp.where(head // KV_GROUP == g, og, o)
    o_ref[...] = o


def _attend_sample_call(page_table, q3, kn3, vn3, bias3, cache_k, cache_v, layer):
    n, n_pages = page_table.shape
    page = cache_k.shape[2] // N_KV_HEADS
    past = n_pages * page
    kern = functools.partial(_attend_sample_kernel, layer=layer, n_pages=n_pages, page=page)
    per = lambda b, pt: (b, 0, 0)
    return pl.pallas_call(
        kern,
        grid_spec=pltpu.PrefetchScalarGridSpec(
            num_scalar_prefetch=1,
            grid=(n,),
            in_specs=[
                pl.BlockSpec((None, N_HEADS, HEAD_DIM), per),
                pl.BlockSpec((None, N_KV_HEADS, HEAD_DIM), per),
                pl.BlockSpec((None, N_KV_HEADS, HEAD_DIM), per),
                pl.BlockSpec((None, 1, past + LANE), per),
                pl.BlockSpec(memory_space=pl.ANY),
                pl.BlockSpec(memory_space=pl.ANY),
            ],
            out_specs=pl.BlockSpec((None, N_HEADS, HEAD_DIM), per),
            scratch_shapes=[
                pltpu.VMEM((2, 1, N_KV_HEADS * past, HEAD_DIM), F32),
                pltpu.VMEM((2, 1, N_KV_HEADS * past, HEAD_DIM), F32),
                pltpu.SemaphoreType.DMA((2,)),
                pltpu.SemaphoreType.DMA((2,)),
            ],
        ),
        out_shape=jax.ShapeDtypeStruct((n, N_HEADS, HEAD_DIM), F32),
        compiler_params=_params(("arbitrary",)),
        name="dsa_attend_sample",
    )(page_table, q3, kn3, vn3, bias3, cache_k, cache_v)


def _out_sample_kernel(x_ref, mod_ref, lng_ref, lnb_ref, o_ref, w_out_ref, y_ref):
    y = _dot(o_ref[...].astype(BF16), w_out_ref[...])
    y_ref[...] = _post_norm(x_ref[...], y, mod_ref[2], lng_ref[...], lnb_ref[...])


def _out_sample_call(x, mod_s, lng, lnb, o, w_out):
    args = (x, mod_s, _row(lng), _row(lnb), o, w_out)
    return pl.pallas_call(
        _out_sample_kernel,
        grid=(1,),
        in_specs=[_resident_spec(a) for a in args],
        out_specs=_full_spec(x.shape),
        out_shape=jax.ShapeDtypeStruct(x.shape, F32),
        compiler_params=_params(("arbitrary",)),
        name="dsa_out_sample",
    )(*_arrays(args))


def _pack_proj_weight(w_in):
    d = w_in.shape[0]
    o_qi = Q_COLS + 2 * KV_COLS
    o_ki = o_qi + QI_COLS
    o_wi = o_ki + IDX_DIM
    qi = w_in[:, o_qi:o_ki].reshape(d, N_IDX_HEADS, IDX_DIM)
    qi = jnp.pad(qi, ((0, 0), (0, 0), (0, LANE - IDX_DIM))).reshape(d, QI_PAD)
    ki = jnp.pad(w_in[:, o_ki:o_wi], ((0, 0), (0, LANE - IDX_DIM)))
    wi = jnp.pad(w_in[:, o_wi:], ((0, 0), (0, LANE - N_IDX_HEADS)))
    return jnp.concatenate([w_in[:, :o_qi], qi, ki, wi], axis=1).astype(BF16)


def _pack_proj_weight_t(w_in):
    o_v = Q_COLS + KV_COLS
    o_wi = Q_COLS + 2 * KV_COLS + QI_COLS + IDX_DIM
    wt = jnp.concatenate([w_in[:, o_v:o_v + KV_COLS], w_in[:, o_wi:]], axis=1).T
    return jnp.pad(wt, ((0, 2 * SUBLANE - N_IDX_HEADS), (0, 0))).astype(BF16)


def kernel(x_prompt, x_sample, cache_k, cache_v, cache_kidx, state_conv, page_table, c_prompt, c_sample,
           w_ada, b_ada, ln_g, ln_b, sgu_w_in, sgu_b_in, sgu_norm_g, sgu_norm_b, sgu_w_s, sgu_b_s, sgu_w_out,
           dsa_w_in, dsa_w_out, ffn_w_up, ffn_conv_w, ffn_conv_b, ffn_w_down):
    bsz, t_p, _ = x_prompt.shape
    n_s = x_sample.shape[0]
    n_phys, page = cache_k.shape[1], cache_k.shape[2]
    past = page_table.shape[1] * page

    c_prompt8 = jnp.pad(c_prompt, ((0, SUBLANE - bsz), (0, 0)))
    mods_s, mods_p = _ada_call(c_sample, c_prompt8, w_ada, b_ada)

    ck = cache_k.reshape(cache_k.shape[0], n_phys, page * N_KV_HEADS, HEAD_DIM)
    cv = cache_v.reshape(cache_v.shape[0], n_phys, page * N_KV_HEADS, HEAD_DIM)
    ckidx_t = jnp.swapaxes(cache_kidx, 2, 3)

    sgu_w_in_b, sgu_w_out_b = sgu_w_in.astype(BF16), sgu_w_out.astype(BF16)
    dsa_w_out_b = dsa_w_out.astype(BF16)
    ffn_w_up_b, ffn_w_down_b = ffn_w_up.astype(BF16), ffn_w_down.astype(BF16)
    n_dsa = DEPTH // 2

    xp = x_prompt
    xs = x_sample.reshape(n_s, D_MODEL)
    kv_stacks = tuple(jnp.zeros((n_dsa, bsz, N_KV_HEADS * t_p, HEAD_DIM), F32) for _ in range(2))
    kip_l, ks_l, vs_l, kis_l, sgu_l, convp_l, convs_l = [], [], [], [], [], [], []
    for i in range(DEPTH):
        j = i // 2
        mod_p, mod_s = (mods_p, i), (mods_s, i)
        if i % 2 == 0:
            w_in = (sgu_w_in_b, j)
            w_out = (sgu_w_out_b, j)
            wtril = jnp.tril(sgu_w_s[j]).astype(BF16)
            bs_full = jnp.repeat(sgu_b_s[j].T, SGU_GROUP_DIM, axis=1)
            ws0 = jnp.repeat(sgu_w_s[j][:, 0, 0], SGU_GROUP_DIM)
            bs0 = jnp.repeat(sgu_b_s[j][:, 0], SGU_GROUP_DIM)
            xp = _sgu_prompt_call(xp, mod_p, ln_g[i, 0], ln_b[i, 0], w_in, sgu_b_in[j], sgu_norm_g[j],
                                  sgu_norm_b[j], wtril, bs_full, w_out)
            xs, v_rows = _sgu_sample_call(xs, mod_s, ln_g[i, 0], ln_b[i, 0], w_in, sgu_b_in[j], sgu_norm_g[j],
                                          sgu_norm_b[j], ws0, bs0, w_out)
            sgu_l.append(v_rows.reshape(n_s, 1, D_SGU))
        else:
            w_proj = _pack_proj_weight(dsa_w_in[j])
            w_out = (dsa_w_out_b, j)
            q, k_stack, v_stack, kb, vt, qi_hm, ki, kib, wit = _proj_prompt_call(
                xp, mod_p, w_proj, _pack_proj_weight_t(dsa_w_in[j]), j, n_dsa, kv_stacks)
            kv_stacks = (k_stack, v_stack)
            xp = _attend_prompt_call(xp, mod_p, ln_g[i, 0], ln_b[i, 0], q, qi_hm, wit, kb, vt, kib, w_out)
            kip_l.append(ki)

            qs, ks_new, vs_new, qis, kis_new, wis, sn = _proj_sample_call(xs, mod_s, w_proj)
            wbc = jnp.broadcast_to(wis[:, :N_IDX_HEADS, None], (n_s, N_IDX_HEADS, LANE))
            bias = _select_sample_call(page_table, qis.reshape(n_s, N_IDX_HEADS, LANE), wbc, sn, ckidx_t, j)
            o = _attend_sample_call(page_table, qs.reshape(n_s, N_HEADS, HEAD_DIM),
                                    ks_new.reshape(n_s, N_KV_HEADS, HEAD_DIM),
                                    vs_new.reshape(n_s, N_KV_HEADS, HEAD_DIM),
                                    bias.reshape(n_s, 1, past + LANE), ck, cv, j)
            xs = _out_sample_call(xs, mod_s, ln_g[i, 0], ln_b[i, 0], o.reshape(n_s, Q_COLS), w_out)
            ks_l.append(ks_new.reshape(n_s, 1, N_KV_HEADS, HEAD_DIM))
            vs_l.append(vs_new.reshape(n_s, 1, N_KV_HEADS, HEAD_DIM))
            kis_l.append(kis_new.reshape(n_s, 1, IDX_DIM))

        w_up = (ffn_w_up_b, i)
        w_down = (ffn_w_down_b, i)
        xp, conv_p = _ffn_prompt_call(xp, mod_p, ln_g[i, 1], ln_b[i, 1], w_up, ffn_conv_w[i], ffn_conv_b[i], w_down)
        past_t = jnp.swapaxes(state_conv[i], 0, 1)
        xs, a_s = _ffn_sample_call(xs, mod_s, ln_g[i, 1], ln_b[i, 1], w_up, ffn_conv_w[i], ffn_conv_b[i], w_down,
                                   past_t)
        convp_l.append(conv_p)
        convs_l.append(jnp.stack([state_conv[i][:, 1], a_s], axis=1))

    return (xp, xs.reshape(n_s, 1, D_MODEL),
            kv_stacks[0].reshape(n_dsa, bsz, t_p, N_KV_HEADS, HEAD_DIM),
            kv_stacks[1].reshape(n_dsa, bsz, t_p, N_KV_HEADS, HEAD_DIM), jnp.stack(kip_l),
            jnp.stack(ks_l), jnp.stack(vs_l), jnp.stack(kis_l),
            jnp.stack(sgu_l), jnp.stack(convp_l), jnp.stack(convs_l))
```

```python
import functools

import jax
import jax.numpy as jnp
from jax import lax
from jax.experimental import pallas as pl
from jax.experimental.pallas import tpu as pltpu

F32 = jnp.float32
BF16 = jnp.bfloat16
I32 = jnp.int32

D_MODEL = 1024
DEPTH = 4
N_MOD = 6
CHUNK = 128
D_SGU = D_MODEL
SGU_GROUPS = 8
SGU_GROUP_DIM = D_SGU // SGU_GROUPS
N_HEADS = 8
HEAD_DIM = D_MODEL // N_HEADS
N_KV_HEADS = 2
KV_GROUP = N_HEADS // N_KV_HEADS
N_IDX_HEADS = 8
IDX_DIM = 64
TOPK_MAX = 256
Q_COLS = N_HEADS * HEAD_DIM
KV_COLS = N_KV_HEADS * HEAD_DIM
QI_COLS = N_IDX_HEADS * IDX_DIM
D_FF = 2816
CONV_W = 3
ALPHA = (2 * DEPTH) ** 0.25
LN_EPS = 1e-5

LANE = 128
SUBLANE = 8
VMEM_LIMIT = 56 * 1024 * 1024

TM = 512
FF_CHUNK = 256
N_FF_CHUNKS = D_FF // FF_CHUNK
QB = 128
TK = 512
KEY_MIN = -2 ** 31
NEG = -1e30
QI_PAD = N_IDX_HEADS * LANE
PROJ_COLS = Q_COLS + 2 * KV_COLS + QI_PAD + 2 * LANE
SB = 8
LOG2E = 1.4426950408889634
VT_ROWS = HEAD_DIM + 16


def _dot(a, b):
    return jnp.dot(a, b, preferred_element_type=F32)


def _dot_nt(a, b):
    return lax.dot_general(a, b, (((1,), (1,)), ((), ())), preferred_element_type=F32)


def _ln(x):
    mu = jnp.mean(x, axis=-1, keepdims=True)
    xc = x - mu
    var = jnp.mean(xc * xc, axis=-1, keepdims=True)
    return xc * lax.rsqrt(var + LN_EPS)


def _modulate(x, shift, scale):
    return x * (1.0 + scale) + shift


def _post_norm(x, y, gate, g, b):
    return _ln(ALPHA * x + (1.0 + gate) * y) * g + b


def _prompt_mod(mod_ref, m):
    return mod_ref[m, pl.ds(pl.program_id(0), 1), :]


def _key_value(key):
    bits = jnp.where(key < 0, jnp.int32(KEY_MIN) - key, key)
    return lax.bitcast_convert_type(bits, F32)


def _kth_threshold(count_ge_bf16, count_ge, shape, kth):
    def bisect(count, key, top_bit, n_bits):
        def bit_body(it, carry):
            key, cnt = carry
            cand = key + jnp.left_shift(jnp.int32(1), top_bit - it)
            c = jnp.broadcast_to(count(_key_value(cand)), shape)
            ok = c >= kth
            return jnp.where(ok, cand, key), jnp.where(ok, c, cnt)
        return lax.fori_loop(0, n_bits, bit_body, (key, jnp.full(shape, kth, F32)))

    grid_key, _ = bisect(count_ge_bf16, jnp.full(shape, KEY_MIN, I32), 31, 16)
    lo = jnp.where(grid_key == KEY_MIN, KEY_MIN, grid_key - (2 ** 15 + 1))
    key, cnt = bisect(count_ge, lo, 16, 17)
    return jnp.where(key == KEY_MIN, -jnp.inf, _key_value(key)), cnt - kth


def _params(sem=None):
    return pltpu.CompilerParams(dimension_semantics=sem, vmem_limit_bytes=VMEM_LIMIT)


def _const_spec(shape):
    return pl.BlockSpec(shape, lambda *_: (0,) * len(shape), pipeline_mode=pl.Buffered(1))


def _full_spec(shape):
    return pl.BlockSpec(shape, lambda *_: (0,) * len(shape))


def _resident_spec(a):
    if isinstance(a, tuple):
        arr, layer = a
        tail = (0,) * (arr.ndim - 1)
        return pl.BlockSpec((None,) + arr.shape[1:], lambda *_: (layer,) + tail, pipeline_mode=pl.Buffered(1))
    return _const_spec(a.shape)


def _arrays(args):
    return [a[0] if isinstance(a, tuple) else a for a in args]


def _ada_kernel(cs_ref, cp_ref, w_ref, b_ref, os_ref, op_ref):
    w = w_ref[...].astype(BF16)
    bias = b_ref[...]
    os_ref[...] = _dot(jax.nn.silu(cs_ref[...]).astype(BF16), w) + bias
    op_ref[...] = _dot(jax.nn.silu(cp_ref[...]).astype(BF16), w) + bias


def _ada_call(c_sample, c_prompt8, w_ada, b_ada):
    tn = 512
    nn = D_MODEL // tn
    n_s = c_sample.shape[0]
    return pl.pallas_call(
        _ada_kernel,
        grid=(DEPTH, N_MOD, nn),
        in_specs=[
            pl.BlockSpec((n_s, D_MODEL), lambda l, m, n: (0, 0)),
            pl.BlockSpec((SUBLANE, D_MODEL), lambda l, m, n: (0, 0)),
            pl.BlockSpec((None, D_MODEL, tn), lambda l, m, n: (l, 0, m * nn + n)),
            pl.BlockSpec((None, 1, tn), lambda l, m, n: (l, 0, m * nn + n)),
        ],
        out_specs=[
            pl.BlockSpec((None, None, n_s, tn), lambda l, m, n: (l, m, 0, n)),
            pl.BlockSpec((None, None, SUBLANE, tn), lambda l, m, n: (l, m, 0, n)),
        ],
        out_shape=[
            jax.ShapeDtypeStruct((DEPTH, N_MOD, n_s, D_MODEL), F32),
            jax.ShapeDtypeStruct((DEPTH, N_MOD, SUBLANE, D_MODEL), F32),
        ],
        compiler_params=_params(("arbitrary",) * 3),
        name="ada_params",
    )(c_sample, c_prompt8, w_ada, b_ada.reshape(DEPTH, 1, N_MOD * D_MODEL))


def _sgu_front(x, shift, scale, w_in_ref, b_in_ref, ng_ref, nb_ref):
    h = _modulate(x, shift, scale).astype(BF16)
    u = jax.nn.gelu(_dot(h, w_in_ref[:, :D_SGU]) + b_in_ref[:, :D_SGU])
    v = jax.nn.gelu(_dot(h, w_in_ref[:, D_SGU:]) + b_in_ref[:, D_SGU:])
    v = _ln(v) * ng_ref[...] + nb_ref[...]
    return u, v


def _sgu_prompt_kernel(x_ref, mod_ref, lng_ref, lnb_ref, w_in_ref, b_in_ref, ng_ref, nb_ref,
                       wtril_ref, bs_ref, w_out_ref, o_ref, gated_ref):
    x = x_ref[...]
    u, v = _sgu_front(x, _prompt_mod(mod_ref, 0), _prompt_mod(mod_ref, 1),
                      w_in_ref, b_in_ref, ng_ref, nb_ref)
    vb = v.astype(BF16)
    n_chunks = x.shape[0] // CHUNK
    for g in range(SGU_GROUPS):
        cols = slice(g * SGU_GROUP_DIM, (g + 1) * SGU_GROUP_DIM)
        rhs = jnp.concatenate([vb[n * CHUNK:(n + 1) * CHUNK, cols] for n in range(n_chunks)], axis=1)
        mixed = _dot(wtril_ref[g], rhs)
        for n in range(n_chunks):
            rows = slice(n * CHUNK, (n + 1) * CHUNK)
            mix_n = mixed[:, n * SGU_GROUP_DIM:(n + 1) * SGU_GROUP_DIM] + bs_ref[:, cols]
            gated_ref[rows, cols] = (u[rows, cols] * mix_n).astype(BF16)
    y = _dot(gated_ref[...], w_out_ref[...])
    o_ref[...] = _post_norm(x, y, _prompt_mod(mod_ref, 2), lng_ref[...], lnb_ref[...])


def _sgu_sample_kernel(x_ref, mod_ref, lng_ref, lnb_ref, w_in_ref, b_in_ref, ng_ref, nb_ref,
                       ws0_ref, bs0_ref, w_out_ref, o_ref, v_ref):
    x = x_ref[...]
    u, v = _sgu_front(x, mod_ref[0], mod_ref[1], w_in_ref, b_in_ref, ng_ref, nb_ref)
    v_ref[...] = v
    mixed = v * ws0_ref[...] + bs0_ref[...]
    y = _dot((u * mixed).astype(BF16), w_out_ref[...])
    o_ref[...] = _post_norm(x, y, mod_ref[2], lng_ref[...], lnb_ref[...])


def _row(v):
    return v.reshape(1, -1)


def _sgu_prompt_call(x, mod_p, lng, lnb, w_in, b_in, ng, nb, wtril, bs_full, w_out):
    bsz, t, _ = x.shape
    resident = (mod_p, _row(lng), _row(lnb), w_in, _row(b_in), _row(ng), _row(nb), wtril, bs_full, w_out)
    return pl.pallas_call(
        _sgu_prompt_kernel,
        grid=(bsz, t // TM),
        in_specs=[pl.BlockSpec((None, TM, D_MODEL), lambda b, i: (b, i, 0))] + [_resident_spec(a) for a in resident],
        out_specs=pl.BlockSpec((None, TM, D_MODEL), lambda b, i: (b, i, 0)),
        out_shape=jax.ShapeDtypeStruct(x.shape, F32),
        scratch_shapes=[pltpu.VMEM((TM, D_SGU), BF16)],
        compiler_params=_params(("arbitrary", "arbitrary")),
        name="sgu_prompt",
    )(x, *_arrays(resident))


def _sgu_sample_call(x, mod_s, lng, lnb, w_in, b_in, ng, nb, ws0, bs0, w_out):
    n = x.shape[0]
    args = (x, mod_s, _row(lng), _row(lnb), w_in, _row(b_in), _row(ng), _row(nb), _row(ws0), _row(bs0), w_out)
    return pl.pallas_call(
        _sgu_sample_kernel,
        grid=(1,),
        in_specs=[_resident_spec(a) for a in args],
        out_specs=[_full_spec((n, D_MODEL)), _full_spec((n, D_SGU))],
        out_shape=[jax.ShapeDtypeStruct((n, D_MODEL), F32), jax.ShapeDtypeStruct((n, D_SGU), F32)],
        compiler_params=_params(("arbitrary",)),
        name="sgu_sample",
    )(*_arrays(args))


def _ffn_prompt_kernel(x_ref, mod_ref, lng_ref, lnb_ref, w_up_ref, cw_ref, cb_ref, w_down_ref,
                       o_ref, st_ref, carry_ref, g_ref, h_ref):
    i = pl.program_id(1)
    tm = x_ref.shape[0]

    @pl.when(i == 0)
    def _():
        carry_ref[...] = jnp.zeros(carry_ref.shape, F32)

    h_ref[...] = _modulate(x_ref[...], _prompt_mod(mod_ref, 3), _prompt_mod(mod_ref, 4)).astype(BF16)

    def up(c):
        a = _dot(h_ref[...], w_up_ref[:, c * FF_CHUNK:(c + 1) * FF_CHUNK])
        u = _dot(h_ref[...], w_up_ref[:, D_FF + c * FF_CHUNK:D_FF + (c + 1) * FF_CHUNK])
        return a, u

    head_row = lax.broadcasted_iota(I32, (SUBLANE, FF_CHUNK), 0)
    nxt = up(0)
    for c in range(N_FF_CHUNKS):
        cols = slice(c * FF_CHUNK, (c + 1) * FF_CHUNK)
        a, u = nxt
        if c + 1 < N_FF_CHUNKS:
            nxt = up(c + 1)
        prev = carry_ref[c]
        shifted = []
        for j in range(1, CONV_W):
            rolled = pltpu.roll(a, j, axis=0)
            head = jnp.where(head_row < j, pltpu.roll(prev, j, axis=0), rolled[0:SUBLANE, :])
            shifted.append(jnp.concatenate([head, rolled[SUBLANE:, :]], axis=0))
        a_m1, a_m2 = shifted
        conv = a_m2 * cw_ref[0:1, cols] + a_m1 * cw_ref[1:2, cols] + a * cw_ref[2:3, cols] + cb_ref[:, cols]
        carry_ref[c] = a[tm - SUBLANE:tm, :]
        g_ref[:, cols] = (jax.nn.gelu(conv) * u).astype(BF16)

    @pl.when(i == pl.num_programs(1) - 1)
    def _():
        for c in range(N_FF_CHUNKS):
            st_ref[:, c * FF_CHUNK:(c + 1) * FF_CHUNK] = carry_ref[c, SUBLANE - (CONV_W - 1):SUBLANE, :]

    gate, lng, lnb = _prompt_mod(mod_ref, 5), lng_ref[...], lnb_ref[...]
    half = tm // 2
    for r in range(2):
        rows = slice(r * half, (r + 1) * half)
        y = _dot(g_ref[rows, :], w_down_ref[...])
        o_ref[rows, :] = _post_norm(x_ref[rows, :], y, gate, lng, lnb)


def _ffn_sample_kernel(x_ref, mod_ref, lng_ref, lnb_ref, w_up_ref, cw_ref, cb_ref, w_down_ref, past_ref,
                       o_ref, a_ref, g_ref):
    x = x_ref[...]
    h = _modulate(x, mod_ref[3], mod_ref[4]).astype(BF16)
    for c in range(N_FF_CHUNKS):
        cols = slice(c * FF_CHUNK, (c + 1) * FF_CHUNK)
        ucols = slice(D_FF + c * FF_CHUNK, D_FF + (c + 1) * FF_CHUNK)
        a = _dot(h, w_up_ref[:, cols])
        a_ref[:, cols] = a
        conv = (past_ref[0, :, cols] * cw_ref[0:1, cols] + past_ref[1, :, cols] * cw_ref[1:2, cols]
                + a * cw_ref[2:3, cols] + cb_ref[:, cols])
        u = _dot(h, w_up_ref[:, ucols])
        g_ref[:, cols] = (jax.nn.gelu(conv) * u).astype(BF16)
    y = _dot(g_ref[...], w_down_ref[...])
    o_ref[...] = _post_norm(x, y, mod_ref[5], lng_ref[...], lnb_ref[...])


def _ffn_prompt_call(x, mod_p, lng, lnb, w_up, cw, cb, w_down):
    bsz, t, _ = x.shape
    resident = (mod_p, _row(lng), _row(lnb), w_up, cw, _row(cb), w_down)
    return pl.pallas_call(
        _ffn_prompt_kernel,
        grid=(bsz, t // TM),
        in_specs=[pl.BlockSpec((None, TM, D_MODEL), lambda b, i: (b, i, 0))] + [_resident_spec(a) for a in resident],
        out_specs=[
            pl.BlockSpec((None, TM, D_MODEL), lambda b, i: (b, i, 0)),
            pl.BlockSpec((None, CONV_W - 1, D_FF), lambda b, i: (b, 0, 0)),
        ],
        out_shape=[jax.ShapeDtypeStruct(x.shape, F32), jax.ShapeDtypeStruct((bsz, CONV_W - 1, D_FF), F32)],
        scratch_shapes=[pltpu.VMEM((N_FF_CHUNKS, SUBLANE, FF_CHUNK), F32), pltpu.VMEM((TM, D_FF), BF16),
                        pltpu.VMEM((TM, D_MODEL), BF16)],
        compiler_params=_params(("arbitrary", "arbitrary")),
        name="ffn_prompt",
    )(x, *_arrays(resident))


def _ffn_sample_call(x, mod_s, lng, lnb, w_up, cw, cb, w_down, past_t):
    n = x.shape[0]
    args = (x, mod_s, _row(lng), _row(lnb), w_up, cw, _row(cb), w_down, past_t)
    return pl.pallas_call(
        _ffn_sample_kernel,
        grid=(1,),
        in_specs=[_resident_spec(a) for a in args],
        out_specs=[_full_spec((n, D_MODEL)), _full_spec((n, D_FF))],
        out_shape=[jax.ShapeDtypeStruct((n, D_MODEL), F32), jax.ShapeDtypeStruct((n, D_FF), F32)],
        scratch_shapes=[pltpu.VMEM((n, D_FF), BF16)],
        compiler_params=_params(("arbitrary",)),
        name="ffn_sample",
    )(*_arrays(args))


_C_K = Q_COLS
_C_V = _C_K + KV_COLS
_C_QI = _C_V + KV_COLS
_C_KI = _C_QI + QI_PAD
_C_WI = _C_KI + LANE


def _proj_common(h, w_ref):
    q = _dot(h, w_ref[:, 0:_C_K]) * HEAD_DIM ** -0.5
    k = _dot(h, w_ref[:, _C_K:_C_V])
    v = _dot(h, w_ref[:, _C_V:_C_QI])
    qi = _dot(h, w_ref[:, _C_QI:_C_KI])
    ki = _dot(h, w_ref[:, _C_KI:_C_WI])
    wi = _dot(h, w_ref[:, _C_WI:PROJ_COLS]) * N_IDX_HEADS ** -0.5 * IDX_DIM ** -0.5
    return q, k, v, qi, ki, wi


def _store_kv_rows(ref, x):
    n = x.shape[0]
    for g in range(N_KV_HEADS):
        ref[pl.ds(g, n, stride=N_KV_HEADS), :] = x[:, g * HEAD_DIM:(g + 1) * HEAD_DIM]


def _proj_prompt_kernel(x_ref, mod_ref, w_ref, wt_ref, *refs, n_carried):
    q_ref, k_ref, v_ref, kb_ref, vt_ref, qi_ref, ki_ref, kib_ref, wit_ref = refs[n_carried:]
    h = _modulate(x_ref[...], _prompt_mod(mod_ref, 0), _prompt_mod(mod_ref, 1)).astype(BF16)
    q, k, v, qi, ki, _ = _proj_common(h, w_ref)
    q_ref[...] = (q * LOG2E).astype(BF16)
    _store_kv_rows(k_ref, k)
    _store_kv_rows(v_ref, v)
    kb_ref[...] = k.astype(BF16)
    for hh in range(N_IDX_HEADS):
        qi_ref[hh] = qi[:, hh * LANE:(hh + 1) * LANE].astype(BF16)
    ki_ref[...] = ki[:, :IDX_DIM]
    kib_ref[...] = ki.astype(BF16)
    vt = _dot_nt(wt_ref[0:KV_COLS, :], h).astype(BF16)
    ones_rows = (lax.broadcasted_iota(I32, (VT_ROWS - HEAD_DIM, vt.shape[1]), 0) == 0).astype(BF16)
    for g in range(N_KV_HEADS):
        vt_ref[0, g * VT_ROWS:g * VT_ROWS + HEAD_DIM, :] = vt[g * HEAD_DIM:(g + 1) * HEAD_DIM, :]
        vt_ref[0, g * VT_ROWS + HEAD_DIM:(g + 1) * VT_ROWS, :] = ones_rows
    wit = _dot_nt(wt_ref[KV_COLS:, :], h) * N_IDX_HEADS ** -0.5 * IDX_DIM ** -0.5
    wit_ref[...] = wit[0:N_IDX_HEADS, :]


def _proj_sample_kernel(x_ref, mod_ref, w_ref, q_ref, k_ref, v_ref, qi_ref, ki_ref, wi_ref, sn_ref):
    h = _modulate(x_ref[...], mod_ref[0], mod_ref[1]).astype(BF16)
    q, k, v, qi, ki, wi = _proj_common(h, w_ref)
    q_ref[...] = q
    _store_kv_rows(k_ref, k)
    _store_kv_rows(v_ref, v)
    qi_ref[...] = qi
    ki_ref[...] = ki[:, :IDX_DIM]
    wi_ref[...] = wi
    kr = ki.astype(BF16).astype(F32)
    sn = jnp.zeros((x_ref.shape[0], 1), F32)
    for hh in range(N_IDX_HEADS):
        qr = qi[:, hh * LANE:(hh + 1) * LANE].astype(BF16).astype(F32)
        sh = jnp.sum(qr * kr, axis=1, keepdims=True)
        sn = sn + jnp.maximum(sh, 0.0) * wi[:, hh:hh + 1]
    sn_ref[...] = jnp.broadcast_to(sn, sn_ref.shape)


def _proj_prompt_call(x, mod_p, w_proj, w_proj_t, slot, n_slots, kv_stacks):
    bsz, t, _ = x.shape
    assert TM == TK
    tok = lambda b, i: (b, i, 0)
    resident = (mod_p, w_proj, w_proj_t)
    carried = tuple(kv_stacks)
    kv_spec = pl.BlockSpec((None, None, N_KV_HEADS * TM, HEAD_DIM), lambda b, i: (slot, b, i, 0))
    kv_shape = jax.ShapeDtypeStruct((n_slots, bsz, N_KV_HEADS * t, HEAD_DIM), F32)
    return pl.pallas_call(
        functools.partial(_proj_prompt_kernel, n_carried=len(carried)),
        grid=(bsz, t // TM),
        in_specs=([pl.BlockSpec((None, TM, D_MODEL), tok)] + [_resident_spec(a) for a in resident]
                  + [pl.BlockSpec(memory_space=pl.ANY)] * len(carried)),
        input_output_aliases={1 + len(resident) + n: 1 + n for n in range(len(carried))},
        out_specs=[
            pl.BlockSpec((None, TM, Q_COLS), tok),
            kv_spec, kv_spec,
            pl.BlockSpec((None, TM, KV_COLS), tok),
            pl.BlockSpec((None, 1, N_KV_HEADS * VT_ROWS, TK), lambda b, i: (b, i, 0, 0)),
            pl.BlockSpec((None, N_IDX_HEADS, TM, LANE), lambda b, i: (b, 0, i, 0)),
            pl.BlockSpec((None, TM, IDX_DIM), tok),
            pl.BlockSpec((None, TM, LANE), tok),
            pl.BlockSpec((None, N_IDX_HEADS, TM), lambda b, i: (b, 0, i)),
        ],
        out_shape=[
            jax.ShapeDtypeStruct((bsz, t, Q_COLS), BF16),
            kv_shape, kv_shape,
            jax.ShapeDtypeStruct((bsz, t, KV_COLS), BF16),
            jax.ShapeDtypeStruct((bsz, t // TK, N_KV_HEADS * VT_ROWS, TK), BF16),
            jax.ShapeDtypeStruct((bsz, N_IDX_HEADS, t, LANE), BF16),
            jax.ShapeDtypeStruct((bsz, t, IDX_DIM), F32),
            jax.ShapeDtypeStruct((bsz, t, LANE), BF16),
            jax.ShapeDtypeStruct((bsz, N_IDX_HEADS, t), F32),
        ],
        compiler_params=_params(("arbitrary", "arbitrary")),
        name="dsa_proj_prompt",
    )(x, *_arrays(resident), *carried)


def _proj_sample_call(x, mod_s, w_proj):
    n = x.shape[0]
    args = (x, mod_s, w_proj)
    kv_shape = (N_KV_HEADS * n, HEAD_DIM)
    shapes = ((n, Q_COLS), kv_shape, kv_shape, (n, QI_PAD), (n, IDX_DIM), (n, LANE), (n, LANE))
    return pl.pallas_call(
        _proj_sample_kernel,
        grid=(1,),
        in_specs=[_resident_spec(a) for a in args],
        out_specs=[_full_spec(s) for s in shapes],
        out_shape=[jax.ShapeDtypeStruct(s, F32) for s in shapes],
        compiler_params=_params(("arbitrary",)),
        name="dsa_proj_sample",
    )(*_arrays(args))


def _attend_prompt_kernel(x_ref, mod_ref, lng_ref, lnb_ref, q_ref, qi_ref, wit_ref, kb_ref, vt_ref, kib_ref,
                          w_out_ref, o_ref, sc_ref, sc16_ref, q4_ref, acc_ref):
    i = pl.program_id(1)
    n_chunks = (i * QB) // TK + 1
    sub_tiles = TK // SUBLANE

    for g in range(N_KV_HEADS):
        for hh in range(KV_GROUP):
            head = g * KV_GROUP + hh
            q4_ref[g, hh * QB:(hh + 1) * QB, :] = q_ref[:, head * HEAD_DIM:(head + 1) * HEAD_DIM]

    k_pos = lax.broadcasted_iota(I32, (TK, QB), 0)
    q_pos = i * QB + lax.broadcasted_iota(I32, (TK, QB), 1)

    def score_body(c, carry):
        start = pl.multiple_of(c * TK, TK)
        kslab = kib_ref[pl.ds(start, TK), :]
        sidx = jnp.zeros((TK, QB), F32)
        for pair in range(N_IDX_HEADS // 2):
            s = _dot_nt(kslab, qi_ref[2 * pair:2 * pair + 2].reshape(2 * QB, LANE))
            for j in range(2):
                hh = 2 * pair + j
                sidx = sidx + jnp.maximum(s[:, j * QB:(j + 1) * QB], 0.0) * wit_ref[hh:hh + 1, :]
        sidx = jnp.where(k_pos + start <= q_pos, sidx, jnp.nan)
        sc_ref[c] = sidx
        sc16_ref[c] = sidx.astype(BF16)
        return carry

    lax.fori_loop(0, n_chunks, score_body, 0)

    def count_ge_bf16(cand):
        pack = 2 * SUBLANE
        cand16 = jnp.concatenate([cand, cand], axis=0).astype(BF16)

        def body(c, accs):
            kc = sc16_ref[c]
            accs = list(accs)
            for t in range(TK // pack):
                hit = jnp.where(kc[t * pack:(t + 1) * pack, :] >= cand16, jnp.ones((), BF16), jnp.zeros((), BF16))
                accs[t % len(accs)] = accs[t % len(accs)] + hit
            return tuple(accs)

        zero = jnp.zeros((pack, QB), BF16)
        a0, a1, a2, a3 = lax.fori_loop(0, n_chunks, body, (zero,) * 4)
        return jnp.sum(((a0 + a1) + (a2 + a3)).astype(F32), axis=0, keepdims=True)

    def count_ge(cand):
        def body(c, accs):
            kc = sc_ref[c]
            accs = list(accs)
            for t in range(sub_tiles):
                hit = jnp.where(kc[t * SUBLANE:(t + 1) * SUBLANE, :] >= cand, 1.0, 0.0)
                accs[t % len(accs)] = accs[t % len(accs)] + hit
            return tuple(accs)

        zero = jnp.zeros((SUBLANE, QB), F32)
        a0, a1, a2, a3 = lax.fori_loop(0, n_chunks, body, (zero,) * 4)
        return jnp.sum((a0 + a1) + (a2 + a3), axis=0, keepdims=True)

    assert sc_ref.shape[0] * TK // (2 * SUBLANE) <= 256
    thr, excess = _kth_threshold(count_ge_bf16, count_ge, (SUBLANE, QB), TOPK_MAX)
    thr_full = jnp.concatenate([thr] * sub_tiles, axis=0)

    @pl.when(jnp.max(excess) > 0.0)
    def _():
        later_or_same = (lax.broadcasted_iota(I32, (TK, TK), 1) >= lax.broadcasted_iota(I32, (TK, TK), 0)).astype(BF16)

        def drop_body(j, seen):
            c = n_chunks - 1 - j
            kc = sc_ref[c]
            tied = kc == thr_full
            rank_from_end = _dot(later_or_same, jnp.where(tied, 1.0, 0.0).astype(BF16)) + seen
            sc_ref[c] = jnp.where(tied & (rank_from_end <= excess[0:1, :]), jnp.nan, kc)
            return rank_from_end[0:1, :]

        lax.fori_loop(0, n_chunks, drop_body, jnp.zeros((1, QB), F32))

    acc_ref[...] = jnp.zeros(acc_ref.shape, F32)

    def attend_chunks(chunks, m):
        s = []
        for c in chunks:
            start = pl.multiple_of(c * TK, TK)
            bias = jnp.where(sc_ref[c] >= thr_full, 0.0, NEG)
            bias4 = jnp.concatenate([bias] * KV_GROUP, axis=1)
            s.append([_dot_nt(kb_ref[pl.ds(start, TK), g * HEAD_DIM:(g + 1) * HEAD_DIM], q4_ref[g]) + bias4
                      for g in range(N_KV_HEADS)])
        m = list(m)
        for c, s_c in zip(chunks, s):
            for g in range(N_KV_HEADS):
                m_new = jnp.maximum(m[g], jnp.max(s_c[g], axis=0, keepdims=True))
                p = jnp.exp2(s_c[g] - m_new).astype(BF16)
                pv = _dot(vt_ref[c, g * VT_ROWS:(g + 1) * VT_ROWS, :], p)
                acc_ref[g] = jnp.exp2(m[g] - m_new) * acc_ref[g] + pv
                m[g] = m_new
        return tuple(m)

    m = (jnp.full((1, KV_GROUP * QB), NEG, F32),) * N_KV_HEADS
    m = lax.fori_loop(0, n_chunks // 2, lambda j, m: attend_chunks((2 * j, 2 * j + 1), m), m)
    lax.fori_loop(0, n_chunks % 2, lambda _, m: attend_chunks((n_chunks - 1,), m), m)

    heads = []
    for g in range(N_KV_HEADS):
        og = acc_ref[g, 0:HEAD_DIM, :] / acc_ref[g, HEAD_DIM:HEAD_DIM + 1, :]
        heads += [og[:, hh * QB:(hh + 1) * QB].T for hh in range(KV_GROUP)]
    o = jnp.concatenate(heads, axis=1).astype(BF16)
    y = _dot(o, w_out_ref[...])
    o_ref[...] = _post_norm(x_ref[...], y, _prompt_mod(mod_ref, 2), lng_ref[...], lnb_ref[...])


def _attend_prompt_call(x, mod_p, lng, lnb, q, qi_hm, wit, kb, vt, kib, w_out):
    bsz, t, _ = x.shape
    tok = lambda b, i: (b, i, 0)
    seq = lambda b, i: (b, 0, 0)
    return pl.pallas_call(
        _attend_prompt_kernel,
        grid=(bsz, t // QB),
        in_specs=[
            pl.BlockSpec((None, QB, D_MODEL), tok),
            _resident_spec(mod_p), _const_spec((1, D_MODEL)), _const_spec((1, D_MODEL)),
            pl.BlockSpec((None, QB, Q_COLS), tok),
            pl.BlockSpec((None, N_IDX_HEADS, QB, LANE), lambda b, i: (b, 0, i, 0)),
            pl.BlockSpec((None, N_IDX_HEADS, QB), lambda b, i: (b, 0, i)),
            pl.BlockSpec((None, t, KV_COLS), seq),
            pl.BlockSpec((None, t // TK, N_KV_HEADS * VT_ROWS, TK), lambda b, i: (b, 0, 0, 0)),
            pl.BlockSpec((None, t, LANE), seq),
            _resident_spec(w_out),
        ],
        out_specs=pl.BlockSpec((None, QB, D_MODEL), tok),
        out_shape=jax.ShapeDtypeStruct(x.shape, F32),
        scratch_shapes=[
            pltpu.VMEM((t // TK, TK, QB), F32),
            pltpu.VMEM((t // TK, TK, QB), BF16),
            pltpu.VMEM((N_KV_HEADS, KV_GROUP * QB, HEAD_DIM), BF16),
            pltpu.VMEM((N_KV_HEADS, VT_ROWS, KV_GROUP * QB), F32),
        ],
        compiler_params=_params(("arbitrary", "arbitrary")),
        name="dsa_attend_prompt",
    )(x, *_arrays((mod_p, _row(lng), _row(lnb), q, qi_hm, wit, kb, vt, kib, w_out)))


def _page_copies(pt_ref, cache_ref, layer, buf_ref, sem_ref, slot, first_sample, n_samples, n_pages, along_lanes=False):
    rows, cols = cache_ref.shape[2:]
    copies = []
    for bb in range(n_samples):
        for p in range(n_pages):
            src = cache_ref.at[layer, pt_ref[first_sample + bb, p]]
            if along_lanes:
                dst = buf_ref.at[slot, bb, :, pl.ds(p * cols, cols)]
            else:
                dst = buf_ref.at[slot, bb, pl.ds(p * rows, rows), :]
            copies.append(pltpu.make_async_copy(src, dst, sem_ref.at[slot]))
    return copies


def _select_sample_kernel(pt_ref, qi_ref, wbc_ref, sn_ref, cache_ref, bias_ref, buf_ref, sc_ref, sem_ref,
                          *, layer, n_pages, page):
    step = pl.program_id(0)
    slot = step % 2
    past = n_pages * page
    width = past + LANE

    def copies(s, sl):
        return _page_copies(pt_ref, cache_ref, layer, buf_ref, sem_ref, sl, s * SB, SB, n_pages, along_lanes=True)

    @pl.when(step == 0)
    def _():
        for cp in copies(0, 0):
            cp.start()

    @pl.when(step + 1 < pl.num_programs(0))
    def _():
        for cp in copies(step + 1, 1 - slot):
            cp.start()

    for cp in copies(step, slot):
        cp.wait()

    for bb in range(SB):
        qs = qi_ref[bb][:, :IDX_DIM].astype(BF16)
        s = _dot(qs, buf_ref[slot, bb].astype(BF16))
        w = jnp.concatenate([wbc_ref[bb]] * (past // LANE), axis=1)
        sidx = jnp.sum(jnp.maximum(s, 0.0) * w, axis=0, keepdims=True)
        sc_ref[bb:bb + 1, 0:past] = sidx
    lane = lax.broadcasted_iota(I32, (SB, LANE), 1)
    sc_ref[:, past:width] = jnp.where(lane == 0, sn_ref[...], jnp.nan)

    lane_tiles = width // LANE
    sc = sc_ref[...]

    def count_ge(cand):
        acc = jnp.zeros((SB, LANE), F32)
        for t in range(lane_tiles):
            acc = acc + jnp.where(sc[:, t * LANE:(t + 1) * LANE] >= cand, 1.0, 0.0)
        return jnp.sum(acc, axis=1, keepdims=True)

    thr, excess = _kth_threshold(count_ge, count_ge, (SB, LANE), TOPK_MAX)
    thr_full = jnp.concatenate([thr] * lane_tiles, axis=1)
    bias_ref[...] = jnp.where(sc >= thr_full, 0.0, NEG)

    @pl.when(jnp.max(excess) > 0.0)
    def _():
        later_or_same = (lax.broadcasted_iota(I32, (LANE, LANE), 0) >= lax.broadcasted_iota(I32, (LANE, LANE), 1)).astype(BF16)
        seen = jnp.zeros((SB, 1), F32)
        for t in reversed(range(lane_tiles)):
            cols = slice(t * LANE, (t + 1) * LANE)
            tied = sc[:, cols] == thr
            rank_from_end = _dot(jnp.where(tied, 1.0, 0.0).astype(BF16), later_or_same) + seen
            bias_ref[:, cols] = jnp.where(tied & (rank_from_end <= excess), NEG, bias_ref[:, cols])
            seen = rank_from_end[:, 0:1]


def _select_sample_call(page_table, qi3, wbc, sn, cache_kidx_t, layer):
    n, n_pages = page_table.shape
    page = cache_kidx_t.shape[3]
    width = n_pages * page + LANE
    kern = functools.partial(_select_sample_kernel, layer=layer, n_pages=n_pages, page=page)
    return pl.pallas_call(
        kern,
        grid_spec=pltpu.PrefetchScalarGridSpec(
            num_scalar_prefetch=1,
            grid=(n // SB,),
            in_specs=[
                pl.BlockSpec((SB, N_IDX_HEADS, LANE), lambda s, pt: (s, 0, 0)),
                pl.BlockSpec((SB, N_IDX_HEADS, LANE), lambda s, pt: (s, 0, 0)),
                pl.BlockSpec((SB, LANE), lambda s, pt: (s, 0)),
                pl.BlockSpec(memory_space=pl.ANY),
            ],
            out_specs=pl.BlockSpec((SB, width), lambda s, pt: (s, 0)),
            scratch_shapes=[
                pltpu.VMEM((2, SB, IDX_DIM, n_pages * page), F32),
                pltpu.VMEM((SB, width), F32),
                pltpu.SemaphoreType.DMA((2,)),
            ],
        ),
        out_shape=jax.ShapeDtypeStruct((n, width), F32),
        compiler_params=_params(("arbitrary",)),
        name="dsa_select_sample",
    )(page_table, qi3, wbc, sn, cache_kidx_t)


def _attend_sample_kernel(pt_ref, q_ref, kn_ref, vn_ref, bias_ref, ck_ref, cv_ref, o_ref,
                          kbuf_ref, vbuf_ref, ksem_ref, vsem_ref, *, layer, n_pages, page):
    b = pl.program_id(0)
    slot = b % 2
    past = n_pages * page

    def copies(sample, sl):
        return (_page_copies(pt_ref, ck_ref, layer, kbuf_ref, ksem_ref, sl, sample, 1, n_pages)
                + _page_copies(pt_ref, cv_ref, layer, vbuf_ref, vsem_ref, sl, sample, 1, n_pages))

    @pl.when(b == 0)
    def _():
        for cp in copies(0, 0):
            cp.start()

    @pl.when(b + 1 < pl.num_programs(0))
    def _():
        for cp in copies(b + 1, 1 - slot):
            cp.start()

    for cp in copies(b, slot):
        cp.wait()

    qb = q_ref[...].astype(BF16)
    qr = qb.astype(F32)
    bias = bias_ref[...]
    head = lax.broadcasted_iota(I32, (N_HEADS, HEAD_DIM), 0)
    rows = [pl.ds(g, past, stride=N_KV_HEADS) for g in range(N_KV_HEADS)]
    s = [_dot_nt(qb, kbuf_ref[slot, 0, rows[g], :].astype(BF16)) + bias[:, :past] for g in range(N_KV_HEADS)]
    p, p_new, denom = [], [], []
    for g in range(N_KV_HEADS):
        k_new = kn_ref[g:g + 1, :].astype(BF16).astype(F32)
        s_new = jnp.sum(qr * k_new, axis=1, keepdims=True) + bias[:, past:past + 1]
        m = jnp.maximum(jnp.max(s[g], axis=1, keepdims=True), s_new)
        p.append(jnp.exp(s[g] - m))
        p_new.append(jnp.exp(s_new - m))
        denom.append(jnp.sum(p[g], axis=1, keepdims=True) + p_new[g])
    og = [_dot(p[g].astype(BF16), vbuf_ref[slot, 0, rows[g], :].astype(BF16)) for g in range(N_KV_HEADS)]
    o = jnp.zeros((N_HEADS, HEAD_DIM), F32)
    for g in range(N_KV_HEADS):
        v_new = vn_ref[g:g + 1, :].astype(BF16).astype(F32)
        o_g = (og[g] + p_new[g].astype(BF16).astype(F32) * v_new) / denom[g]
        o = jnp.where(head // KV_GROUP == g, o_g, o)
    o_ref[...] = o


def _attend_sample_call(page_table, q3, kn3, vn3, bias3, cache_k, cache_v, layer):
    n, n_pages = page_table.shape
    page = cache_k.shape[2] // N_KV_HEADS
    past = n_pages * page
    kern = functools.partial(_attend_sample_kernel, layer=layer, n_pages=n_pages, page=page)
    per = lambda b, pt: (b, 0, 0)
    return pl.pallas_call(
        kern,
        grid_spec=pltpu.PrefetchScalarGridSpec(
            num_scalar_prefetch=1,
            grid=(n,),
            in_specs=[
                pl.BlockSpec((None, N_HEADS, HEAD_DIM), per),
                pl.BlockSpec((None, N_KV_HEADS, HEAD_DIM), per),
                pl.BlockSpec((None, N_KV_HEADS, HEAD_DIM), per),
                pl.BlockSpec((None, 1, past + LANE), per),
                pl.BlockSpec(memory_space=pl.ANY),
                pl.BlockSpec(memory_space=pl.ANY),
            ],
            out_specs=pl.BlockSpec((None, N_HEADS, HEAD_DIM), per),
            scratch_shapes=[
                pltpu.VMEM((2, 1, N_KV_HEADS * past, HEAD_DIM), F32),
                pltpu.VMEM((2, 1, N_KV_HEADS * past, HEAD_DIM), F32),
                pltpu.SemaphoreType.DMA((2,)),
                pltpu.SemaphoreType.DMA((2,)),
            ],
        ),
        out_shape=jax.ShapeDtypeStruct((n, N_HEADS, HEAD_DIM), F32),
        compiler_params=_params(("arbitrary",)),
        name="dsa_attend_sample",
    )(page_table, q3, kn3, vn3, bias3, cache_k, cache_v)


def _out_sample_kernel(x_ref, mod_ref, lng_ref, lnb_ref, o_ref, w_out_ref, y_ref):
    y = _dot(o_ref[...].astype(BF16), w_out_ref[...])
    y_ref[...] = _post_norm(x_ref[...], y, mod_ref[2], lng_ref[...], lnb_ref[...])


def _out_sample_call(x, mod_s, lng, lnb, o, w_out):
    args = (x, mod_s, _row(lng), _row(lnb), o, w_out)
    return pl.pallas_call(
        _out_sample_kernel,
        grid=(1,),
        in_specs=[_resident_spec(a) for a in args],
        out_specs=_full_spec(x.shape),
        out_shape=jax.ShapeDtypeStruct(x.shape, F32),
        compiler_params=_params(("arbitrary",)),
        name="dsa_out_sample",
    )(*_arrays(args))


def _pack_proj_weight(w_in):
    d = w_in.shape[0]
    o_qi = Q_COLS + 2 * KV_COLS
    o_ki = o_qi + QI_COLS
    o_wi = o_ki + IDX_DIM
    qi = w_in[:, o_qi:o_ki].reshape(d, N_IDX_HEADS, IDX_DIM)
    qi = jnp.pad(qi, ((0, 0), (0, 0), (0, LANE - IDX_DIM))).reshape(d, QI_PAD)
    ki = jnp.pad(w_in[:, o_ki:o_wi], ((0, 0), (0, LANE - IDX_DIM)))
    wi = jnp.pad(w_in[:, o_wi:], ((0, 0), (0, LANE - N_IDX_HEADS)))
    return jnp.concatenate([w_in[:, :o_qi], qi, ki, wi], axis=1).astype(BF16)


def _pack_proj_weight_t(w_in):
    o_v = Q_COLS + KV_COLS
    o_wi = Q_COLS + 2 * KV_COLS + QI_COLS + IDX_DIM
    wt = jnp.concatenate([w_in[:, o_v:o_v + KV_COLS], w_in[:, o_wi:]], axis=1).T
    return jnp.pad(wt, ((0, 2 * SUBLANE - N_IDX_HEADS), (0, 0))).astype(BF16)


def kernel(x_prompt, x_sample, cache_k, cache_v, cache_kidx, state_conv, page_table, c_prompt, c_sample,
           w_ada, b_ada, ln_g, ln_b, sgu_w_in, sgu_b_in, sgu_norm_g, sgu_norm_b, sgu_w_s, sgu_b_s, sgu_w_out,
           dsa_w_in, dsa_w_out, ffn_w_up, ffn_conv_w, ffn_conv_b, ffn_w_down):
    bsz, t_p, _ = x_prompt.shape
    n_s = x_sample.shape[0]
    n_phys, page = cache_k.shape[1], cache_k.shape[2]
    past = page_table.shape[1] * page

    c_prompt8 = jnp.pad(c_prompt, ((0, SUBLANE - bsz), (0, 0)))
    mods_s, mods_p = _ada_call(c_sample, c_prompt8, w_ada, b_ada)

    ck = cache_k.reshape(cache_k.shape[0], n_phys, page * N_KV_HEADS, HEAD_DIM)
    cv = cache_v.reshape(cache_v.shape[0], n_phys, page * N_KV_HEADS, HEAD_DIM)
    ckidx_t = jnp.swapaxes(cache_kidx, 2, 3)

    sgu_w_in_b, sgu_w_out_b = sgu_w_in.astype(BF16), sgu_w_out.astype(BF16)
    dsa_w_out_b = dsa_w_out.astype(BF16)
    ffn_w_up_b, ffn_w_down_b = ffn_w_up.astype(BF16), ffn_w_down.astype(BF16)
    n_dsa = DEPTH // 2

    xp = x_prompt
    xs = x_sample.reshape(n_s, D_MODEL)
    kv_stacks = tuple(jnp.zeros((n_dsa, bsz, N_KV_HEADS * t_p, HEAD_DIM), F32) for _ in range(2))
    kip_l, ks_l, vs_l, kis_l, sgu_l, convp_l, convs_l = [], [], [], [], [], [], []
    for i in range(DEPTH):
        j = i // 2
        mod_p, mod_s = (mods_p, i), (mods_s, i)
        if i % 2 == 0:
            w_in = (sgu_w_in_b, j)
            w_out = (sgu_w_out_b, j)
            wtril = jnp.tril(sgu_w_s[j]).astype(BF16)
            bs_full = jnp.repeat(sgu_b_s[j].T, SGU_GROUP_DIM, axis=1)
            ws0 = jnp.repeat(sgu_w_s[j][:, 0, 0], SGU_GROUP_DIM)
            bs0 = jnp.repeat(sgu_b_s[j][:, 0], SGU_GROUP_DIM)
            xp = _sgu_prompt_call(xp, mod_p, ln_g[i, 0], ln_b[i, 0], w_in, sgu_b_in[j], sgu_norm_g[j],
                                  sgu_norm_b[j], wtril, bs_full, w_out)
            xs, v_rows = _sgu_sample_call(xs, mod_s, ln_g[i, 0], ln_b[i, 0], w_in, sgu_b_in[j], sgu_norm_g[j],
                                          sgu_norm_b[j], ws0, bs0, w_out)
            sgu_l.append(v_rows.reshape(n_s, 1, D_SGU))
        else:
            w_proj = _pack_proj_weight(dsa_w_in[j])
            w_out = (dsa_w_out_b, j)
            q, k_stack, v_stack, kb, vt, qi_hm, ki, kib, wit = _proj_prompt_call(
                xp, mod_p, w_proj, _pack_proj_weight_t(dsa_w_in[j]), j, n_dsa, kv_stacks)
            kv_stacks = (k_stack, v_stack)
            xp = _attend_prompt_call(xp, mod_p, ln_g[i, 0], ln_b[i, 0], q, qi_hm, wit, kb, vt, kib, w_out)
            kip_l.append(ki)

            qs, ks_new, vs_new, qis, kis_new, wis, sn = _proj_sample_call(xs, mod_s, w_proj)
            wbc = jnp.broadcast_to(wis[:, :N_IDX_HEADS, None], (n_s, N_IDX_HEADS, LANE))
            bias = _select_sample_call(page_table, qis.reshape(n_s, N_IDX_HEADS, LANE), wbc, sn, ckidx_t, j)
            o = _attend_sample_call(page_table, qs.reshape(n_s, N_HEADS, HEAD_DIM),
                                    ks_new.reshape(n_s, N_KV_HEADS, HEAD_DIM),
                                    vs_new.reshape(n_s, N_KV_HEADS, HEAD_DIM),
                                    bias.reshape(n_s, 1, past + LANE), ck, cv, j)
            xs = _out_sample_call(xs, mod_s, ln_g[i, 0], ln_b[i, 0], o.reshape(n_s, Q_COLS), w_out)
            ks_l.append(ks_new.reshape(n_s, 1, N_KV_HEADS, HEAD_DIM))
            vs_l.append(vs_new.reshape(n_s, 1, N_KV_HEADS, HEAD_DIM))
            kis_l.append(kis_new.reshape(n_s, 1, IDX_DIM))

        w_up = (ffn_w_up_b, i)
        w_down = (ffn_w_down_b, i)
        xp, conv_p = _ffn_prompt_call(xp, mod_p, ln_g[i, 1], ln_b[i, 1], w_up, ffn_conv_w[i], ffn_conv_b[i], w_down)
        past_t = jnp.swapaxes(state_conv[i], 0, 1)
        xs, a_s = _ffn_sample_call(xs, mod_s, ln_g[i, 1], ln_b[i, 1], w_up, ffn_conv_w[i], ffn_conv_b[i], w_down,
                                   past_t)
        convp_l.append(conv_p)
        convs_l.append(jnp.stack([state_conv[i][:, 1], a_s], axis=1))

    return (xp, xs.reshape(n_s, 1, D_MODEL),
            kv_stacks[0].reshape(n_dsa, bsz, t_p, N_KV_HEADS, HEAD_DIM),
            kv_stacks[1].reshape(n_dsa, bsz, t_p, N_KV_HEADS, HEAD_DIM), jnp.stack(kip_l),
            jnp.stack(ks_l), jnp.stack(vs_l), jnp.stack(kis_l),
            jnp.stack(sgu_l), jnp.stack(convp_l), jnp.stack(convs_l))
```

```python
import functools

import jax
import jax.numpy as jnp
from jax import lax
from jax.experimental import pallas as pl
from jax.experimental.pallas import tpu as pltpu

F32 = jnp.float32
BF16 = jnp.bfloat16
I32 = jnp.int32

D_MODEL = 1024
DEPTH = 4
N_MOD = 6
CHUNK = 128
D_SGU = D_MODEL
SGU_GROUPS = 8
SGU_GROUP_DIM = D_SGU // SGU_GROUPS
N_HEADS = 8
HEAD_DIM = D_MODEL // N_HEADS
N_KV_HEADS = 2
KV_GROUP = N_HEADS // N_KV_HEADS
N_IDX_HEADS = 8
IDX_DIM = 64
TOPK_MAX = 256
Q_COLS = N_HEADS * HEAD_DIM
KV_COLS = N_KV_HEADS * HEAD_DIM
QI_COLS = N_IDX_HEADS * IDX_DIM
D_FF = 2816
CONV_W = 3
ALPHA = (2 * DEPTH) ** 0.25
LN_EPS = 1e-5

LANE = 128
SUBLANE = 8
VMEM_LIMIT = 56 * 1024 * 1024

TM = 512
FF_CHUNK = 256
N_FF_CHUNKS = D_FF // FF_CHUNK
QB = 128
TK = 512
TIE_BLOCK = 256
KEY_MIN = -2 ** 31
NEG = -1e30
QI_PAD = N_IDX_HEADS * LANE
PROJ_COLS = Q_COLS + 2 * KV_COLS + QI_PAD + 2 * LANE
SB = 8
LOG2E = 1.4426950408889634
VT_ROWS = HEAD_DIM + 16


def _dot(a, b):
    return jnp.dot(a, b, preferred_element_type=F32)


def _dot_nt(a, b):
    return lax.dot_general(a, b, (((1,), (1,)), ((), ())), preferred_element_type=F32)


def _ln(x):
    mu = jnp.mean(x, axis=-1, keepdims=True)
    xc = x - mu
    var = jnp.mean(xc * xc, axis=-1, keepdims=True)
    return xc * lax.rsqrt(var + LN_EPS)


def _modulate(x, shift, scale):
    return x * (1.0 + scale) + shift


def _post_norm(x, y, gate, g, b):
    return _ln(ALPHA * x + (1.0 + gate) * y) * g + b


def _prompt_mod(mod_ref, m):
    return mod_ref[m, pl.ds(pl.program_id(0), 1), :]


def _key_value(key):
    bits = jnp.where(key < 0, jnp.int32(KEY_MIN) - key, key)
    return lax.bitcast_convert_type(bits, F32)


def _kth_threshold(count_ge_bf16, count_ge, shape, kth):
    def bisect(count, key, top_bit, n_bits):
        def bit_body(it, carry):
            key, cnt = carry
            cand = key + jnp.left_shift(jnp.int32(1), top_bit - it)
            c = jnp.broadcast_to(count(_key_value(cand)), shape)
            ok = c >= kth
            return jnp.where(ok, cand, key), jnp.where(ok, c, cnt)
        return lax.fori_loop(0, n_bits, bit_body, (key, jnp.full(shape, kth, F32)))

    grid_key, _ = bisect(count_ge_bf16, jnp.full(shape, KEY_MIN, I32), 31, 16)
    lo = jnp.where(grid_key == KEY_MIN, KEY_MIN, grid_key - (2 ** 15 + 1))
    key, cnt = bisect(count_ge, lo, 16, 17)
    return jnp.where(key == KEY_MIN, -jnp.inf, _key_value(key)), cnt - kth


def _params(sem=None):
    return pltpu.CompilerParams(dimension_semantics=sem, vmem_limit_bytes=VMEM_LIMIT)


def _const_spec(shape):
    return pl.BlockSpec(shape, lambda *_: (0,) * len(shape), pipeline_mode=pl.Buffered(1))


def _full_spec(shape):
    return pl.BlockSpec(shape, lambda *_: (0,) * len(shape))


def _resident_spec(a):
    if isinstance(a, tuple):
        arr, layer = a
        tail = (0,) * (arr.ndim - 1)
        return pl.BlockSpec((None,) + arr.shape[1:], lambda *_: (layer,) + tail, pipeline_mode=pl.Buffered(1))
    return _const_spec(a.shape)


def _arrays(args):
    return [a[0] if isinstance(a, tuple) else a for a in args]


def _ada_kernel(cs_ref, cp_ref, w_ref, b_ref, os_ref, op_ref):
    w = w_ref[...].astype(BF16)
    bias = b_ref[...]
    os_ref[...] = _dot(jax.nn.silu(cs_ref[...]).astype(BF16), w) + bias
    op_ref[...] = _dot(jax.nn.silu(cp_ref[...]).astype(BF16), w) + bias


def _ada_call(c_sample, c_prompt8, w_ada, b_ada):
    tn = 512
    nn = D_MODEL // tn
    n_s = c_sample.shape[0]
    return pl.pallas_call(
        _ada_kernel,
        grid=(DEPTH, N_MOD, nn),
        in_specs=[
            pl.BlockSpec((n_s, D_MODEL), lambda l, m, n: (0, 0)),
            pl.BlockSpec((SUBLANE, D_MODEL), lambda l, m, n: (0, 0)),
            pl.BlockSpec((None, D_MODEL, tn), lambda l, m, n: (l, 0, m * nn + n)),
            pl.BlockSpec((None, 1, tn), lambda l, m, n: (l, 0, m * nn + n)),
        ],
        out_specs=[
            pl.BlockSpec((None, None, n_s, tn), lambda l, m, n: (l, m, 0, n)),
            pl.BlockSpec((None, None, SUBLANE, tn), lambda l, m, n: (l, m, 0, n)),
        ],
        out_shape=[
            jax.ShapeDtypeStruct((DEPTH, N_MOD, n_s, D_MODEL), F32),
            jax.ShapeDtypeStruct((DEPTH, N_MOD, SUBLANE, D_MODEL), F32),
        ],
        compiler_params=_params(("arbitrary",) * 3),
        name="ada_params",
    )(c_sample, c_prompt8, w_ada, b_ada.reshape(DEPTH, 1, N_MOD * D_MODEL))


def _sgu_front(x, shift, scale, w_in_ref, b_in_ref, ng_ref, nb_ref):
    h = _modulate(x, shift, scale).astype(BF16)
    u = jax.nn.gelu(_dot(h, w_in_ref[:, :D_SGU]) + b_in_ref[:, :D_SGU])
    v = jax.nn.gelu(_dot(h, w_in_ref[:, D_SGU:]) + b_in_ref[:, D_SGU:])
    v = _ln(v) * ng_ref[...] + nb_ref[...]
    return u, v


def _sgu_prompt_kernel(x_ref, mod_ref, lng_ref, lnb_ref, w_in_ref, b_in_ref, ng_ref, nb_ref,
                       wtril_ref, bs_ref, w_out_ref, o_ref, gated_ref):
    x = x_ref[...]
    u, v = _sgu_front(x, _prompt_mod(mod_ref, 0), _prompt_mod(mod_ref, 1),
                      w_in_ref, b_in_ref, ng_ref, nb_ref)
    vb = v.astype(BF16)
    n_chunks = x.shape[0] // CHUNK
    for g in range(SGU_GROUPS):
        cols = slice(g * SGU_GROUP_DIM, (g + 1) * SGU_GROUP_DIM)
        rhs = jnp.concatenate([vb[n * CHUNK:(n + 1) * CHUNK, cols] for n in range(n_chunks)], axis=1)
        mixed = _dot(wtril_ref[g], rhs)
        for n in range(n_chunks):
            rows = slice(n * CHUNK, (n + 1) * CHUNK)
            mix_n = mixed[:, n * SGU_GROUP_DIM:(n + 1) * SGU_GROUP_DIM] + bs_ref[:, cols]
            gated_ref[rows, cols] = (u[rows, cols] * mix_n).astype(BF16)
    y = _dot(gated_ref[...], w_out_ref[...])
    o_ref[...] = _post_norm(x, y, _prompt_mod(mod_ref, 2), lng_ref[...], lnb_ref[...])


def _sgu_sample_kernel(x_ref, mod_ref, lng_ref, lnb_ref, w_in_ref, b_in_ref, ng_ref, nb_ref,
                       ws0_ref, bs0_ref, w_out_ref, o_ref, v_ref):
    x = x_ref[...]
    u, v = _sgu_front(x, mod_ref[0], mod_ref[1], w_in_ref, b_in_ref, ng_ref, nb_ref)
    v_ref[...] = v
    mixed = v * ws0_ref[...] + bs0_ref[...]
    y = _dot((u * mixed).astype(BF16), w_out_ref[...])
    o_ref[...] = _post_norm(x, y, mod_ref[2], lng_ref[...], lnb_ref[...])


def _row(v):
    return v.reshape(1, -1)


def _sgu_prompt_call(x, mod_p, lng, lnb, w_in, b_in, ng, nb, wtril, bs_full, w_out):
    bsz, t, _ = x.shape
    resident = (mod_p, _row(lng), _row(lnb), w_in, _row(b_in), _row(ng), _row(nb), wtril, bs_full, w_out)
    return pl.pallas_call(
        _sgu_prompt_kernel,
        grid=(bsz, t // TM),
        in_specs=[pl.BlockSpec((None, TM, D_MODEL), lambda b, i: (b, i, 0))] + [_resident_spec(a) for a in resident],
        out_specs=pl.BlockSpec((None, TM, D_MODEL), lambda b, i: (b, i, 0)),
        out_shape=jax.ShapeDtypeStruct(x.shape, F32),
        scratch_shapes=[pltpu.VMEM((TM, D_SGU), BF16)],
        compiler_params=_params(("arbitrary", "arbitrary")),
        name="sgu_prompt",
    )(x, *_arrays(resident))


def _sgu_sample_call(x, mod_s, lng, lnb, w_in, b_in, ng, nb, ws0, bs0, w_out):
    n = x.shape[0]
    args = (x, mod_s, _row(lng), _row(lnb), w_in, _row(b_in), _row(ng), _row(nb), _row(ws0), _row(bs0), w_out)
    return pl.pallas_call(
        _sgu_sample_kernel,
        grid=(1,),
        in_specs=[_resident_spec(a) for a in args],
        out_specs=[_full_spec((n, D_MODEL)), _full_spec((n, D_SGU))],
        out_shape=[jax.ShapeDtypeStruct((n, D_MODEL), F32), jax.ShapeDtypeStruct((n, D_SGU), F32)],
        compiler_params=_params(("arbitrary",)),
        name="sgu_sample",
    )(*_arrays(args))


def _ffn_prompt_kernel(x_ref, mod_ref, lng_ref, lnb_ref, w_up_ref, cw_ref, cb_ref, w_down_ref,
                       o_ref, st_ref, carry_ref, g_ref, h_ref):
    i = pl.program_id(1)
    tm = x_ref.shape[0]

    @pl.when(i == 0)
    def _():
        carry_ref[...] = jnp.zeros(carry_ref.shape, F32)

    h_ref[...] = _modulate(x_ref[...], _prompt_mod(mod_ref, 3), _prompt_mod(mod_ref, 4)).astype(BF16)

    def up(c):
        a = _dot(h_ref[...], w_up_ref[:, c * FF_CHUNK:(c + 1) * FF_CHUNK])
        u = _dot(h_ref[...], w_up_ref[:, D_FF + c * FF_CHUNK:D_FF + (c + 1) * FF_CHUNK])
        return a, u

    head_row = lax.broadcasted_iota(I32, (SUBLANE, FF_CHUNK), 0)
    nxt = up(0)
    for c in range(N_FF_CHUNKS):
        cols = slice(c * FF_CHUNK, (c + 1) * FF_CHUNK)
        a, u = nxt
        if c + 1 < N_FF_CHUNKS:
            nxt = up(c + 1)
        prev = carry_ref[c]
        shifted = []
        for j in range(1, CONV_W):
            rolled = pltpu.roll(a, j, axis=0)
            head = jnp.where(head_row < j, pltpu.roll(prev, j, axis=0), rolled[0:SUBLANE, :])
            shifted.append(jnp.concatenate([head, rolled[SUBLANE:, :]], axis=0))
        a_m1, a_m2 = shifted
        conv = a_m2 * cw_ref[0:1, cols] + a_m1 * cw_ref[1:2, cols] + a * cw_ref[2:3, cols] + cb_ref[:, cols]
        carry_ref[c] = a[tm - SUBLANE:tm, :]
        g_ref[:, cols] = (jax.nn.gelu(conv) * u).astype(BF16)

    @pl.when(i == pl.num_programs(1) - 1)
    def _():
        for c in range(N_FF_CHUNKS):
            st_ref[:, c * FF_CHUNK:(c + 1) * FF_CHUNK] = carry_ref[c, SUBLANE - (CONV_W - 1):SUBLANE, :]

    gate, lng, lnb = _prompt_mod(mod_ref, 5), lng_ref[...], lnb_ref[...]
    half = tm // 2
    for r in range(2):
        rows = slice(r * half, (r + 1) * half)
        y = _dot(g_ref[rows, :], w_down_ref[...])
        o_ref[rows, :] = _post_norm(x_ref[rows, :], y, gate, lng, lnb)


def _ffn_sample_kernel(x_ref, mod_ref, lng_ref, lnb_ref, w_up_ref, cw_ref, cb_ref, w_down_ref, past_ref,
                       o_ref, a_ref, g_ref):
    x = x_ref[...]
    h = _modulate(x, mod_ref[3], mod_ref[4]).astype(BF16)
    for c in range(N_FF_CHUNKS):
        cols = slice(c * FF_CHUNK, (c + 1) * FF_CHUNK)
        ucols = slice(D_FF + c * FF_CHUNK, D_FF + (c + 1) * FF_CHUNK)
        a = _dot(h, w_up_ref[:, cols])
        a_ref[:, cols] = a
        conv = (past_ref[0, :, cols] * cw_ref[0:1, cols] + past_ref[1, :, cols] * cw_ref[1:2, cols]
                + a * cw_ref[2:3, cols] + cb_ref[:, cols])
        u = _dot(h, w_up_ref[:, ucols])
        g_ref[:, cols] = (jax.nn.gelu(conv) * u).astype(BF16)
    y = _dot(g_ref[...], w_down_ref[...])
    o_ref[...] = _post_norm(x, y, mod_ref[5], lng_ref[...], lnb_ref[...])


def _ffn_prompt_call(x, mod_p, lng, lnb, w_up, cw, cb, w_down):
    bsz, t, _ = x.shape
    resident = (mod_p, _row(lng), _row(lnb), w_up, cw, _row(cb), w_down)
    return pl.pallas_call(
        _ffn_prompt_kernel,
        grid=(bsz, t // TM),
        in_specs=[pl.BlockSpec((None, TM, D_MODEL), lambda b, i: (b, i, 0))] + [_resident_spec(a) for a in resident],
        out_specs=[
            pl.BlockSpec((None, TM, D_MODEL), lambda b, i: (b, i, 0)),
            pl.BlockSpec((None, CONV_W - 1, D_FF), lambda b, i: (b, 0, 0)),
        ],
        out_shape=[jax.ShapeDtypeStruct(x.shape, F32), jax.ShapeDtypeStruct((bsz, CONV_W - 1, D_FF), F32)],
        scratch_shapes=[pltpu.VMEM((N_FF_CHUNKS, SUBLANE, FF_CHUNK), F32), pltpu.VMEM((TM, D_FF), BF16),
                        pltpu.VMEM((TM, D_MODEL), BF16)],
        compiler_params=_params(("arbitrary", "arbitrary")),
        name="ffn_prompt",
    )(x, *_arrays(resident))


def _ffn_sample_call(x, mod_s, lng, lnb, w_up, cw, cb, w_down, past_t):
    n = x.shape[0]
    args = (x, mod_s, _row(lng), _row(lnb), w_up, cw, _row(cb), w_down, past_t)
    return pl.pallas_call(
        _ffn_sample_kernel,
        grid=(1,),
        in_specs=[_resident_spec(a) for a in args],
        out_specs=[_full_spec((n, D_MODEL)), _full_spec((n, D_FF))],
        out_shape=[jax.ShapeDtypeStruct((n, D_MODEL), F32), jax.ShapeDtypeStruct((n, D_FF), F32)],
        scratch_shapes=[pltpu.VMEM((n, D_FF), BF16)],
        compiler_params=_params(("arbitrary",)),
        name="ffn_sample",
    )(*_arrays(args))


_C_K = Q_COLS
_C_V = _C_K + KV_COLS
_C_QI = _C_V + KV_COLS
_C_KI = _C_QI + QI_PAD
_C_WI = _C_KI + LANE


def _proj_common(h, w_ref):
    q = _dot(h, w_ref[:, 0:_C_K]) * HEAD_DIM ** -0.5
    k = _dot(h, w_ref[:, _C_K:_C_V])
    v = _dot(h, w_ref[:, _C_V:_C_QI])
    qi = _dot(h, w_ref[:, _C_QI:_C_KI])
    ki = _dot(h, w_ref[:, _C_KI:_C_WI])
    wi = _dot(h, w_ref[:, _C_WI:PROJ_COLS]) * N_IDX_HEADS ** -0.5 * IDX_DIM ** -0.5
    return q, k, v, qi, ki, wi


def _store_kv_rows(ref, x):
    n = x.shape[0]
    for g in range(N_KV_HEADS):
        ref[pl.ds(g, n, stride=N_KV_HEADS), :] = x[:, g * HEAD_DIM:(g + 1) * HEAD_DIM]


def _proj_prompt_kernel(x_ref, mod_ref, w_ref, wt_ref, *refs, n_carried):
    q_ref, k_ref, v_ref, kb_ref, vt_ref, qi_ref, ki_ref, kib_ref, wit_ref = refs[n_carried:]
    h = _modulate(x_ref[...], _prompt_mod(mod_ref, 0), _prompt_mod(mod_ref, 1)).astype(BF16)
    q, k, v, qi, ki, _ = _proj_common(h, w_ref)
    q_ref[...] = (q * LOG2E).astype(BF16)
    _store_kv_rows(k_ref, k)
    _store_kv_rows(v_ref, v)
    kb_ref[...] = k.astype(BF16)
    for hh in range(N_IDX_HEADS):
        qi_ref[hh] = qi[:, hh * LANE:(hh + 1) * LANE].astype(BF16)
    ki_ref[...] = ki[:, :IDX_DIM]
    kib_ref[...] = ki.astype(BF16)
    vt = _dot_nt(wt_ref[0:KV_COLS, :], h).astype(BF16)
    ones_rows = (lax.broadcasted_iota(I32, (VT_ROWS - HEAD_DIM, vt.shape[1]), 0) == 0).astype(BF16)
    for g in range(N_KV_HEADS):
        vt_ref[0, g * VT_ROWS:g * VT_ROWS + HEAD_DIM, :] = vt[g * HEAD_DIM:(g + 1) * HEAD_DIM, :]
        vt_ref[0, g * VT_ROWS + HEAD_DIM:(g + 1) * VT_ROWS, :] = ones_rows
    wit = _dot_nt(wt_ref[KV_COLS:, :], h) * N_IDX_HEADS ** -0.5 * IDX_DIM ** -0.5
    wit_ref[...] = wit[0:N_IDX_HEADS, :]


def _proj_sample_kernel(x_ref, mod_ref, w_ref, q_ref, k_ref, v_ref, qi_ref, ki_ref, wi_ref, sn_ref):
    h = _modulate(x_ref[...], mod_ref[0], mod_ref[1]).astype(BF16)
    q, k, v, qi, ki, wi = _proj_common(h, w_ref)
    q_ref[...] = q
    _store_kv_rows(k_ref, k)
    _store_kv_rows(v_ref, v)
    qi_ref[...] = qi
    ki_ref[...] = ki[:, :IDX_DIM]
    wi_ref[...] = wi
    kr = ki.astype(BF16).astype(F32)
    sn = jnp.zeros((x_ref.shape[0], 1), F32)
    for hh in range(N_IDX_HEADS):
        qr = qi[:, hh * LANE:(hh + 1) * LANE].astype(BF16).astype(F32)
        sh = jnp.sum(qr * kr, axis=1, keepdims=True)
        sn = sn + jnp.maximum(sh, 0.0) * wi[:, hh:hh + 1]
    sn_ref[...] = jnp.broadcast_to(sn, sn_ref.shape)


def _proj_prompt_call(x, mod_p, w_proj, w_proj_t, slot, n_slots, kv_stacks):
    bsz, t, _ = x.shape
    assert TM == TK
    tok = lambda b, i: (b, i, 0)
    resident = (mod_p, w_proj, w_proj_t)
    carried = tuple(kv_stacks)
    kv_spec = pl.BlockSpec((None, None, N_KV_HEADS * TM, HEAD_DIM), lambda b, i: (slot, b, i, 0))
    kv_shape = jax.ShapeDtypeStruct((n_slots, bsz, N_KV_HEADS * t, HEAD_DIM), F32)
    return pl.pallas_call(
        functools.partial(_proj_prompt_kernel, n_carried=len(carried)),
        grid=(bsz, t // TM),
        in_specs=([pl.BlockSpec((None, TM, D_MODEL), tok)] + [_resident_spec(a) for a in resident]
                  + [pl.BlockSpec(memory_space=pl.ANY)] * len(carried)),
        input_output_aliases={1 + len(resident) + n: 1 + n for n in range(len(carried))},
        out_specs=[
            pl.BlockSpec((None, TM, Q_COLS), tok),
            kv_spec, kv_spec,
            pl.BlockSpec((None, TM, KV_COLS), tok),
            pl.BlockSpec((None, 1, N_KV_HEADS * VT_ROWS, TK), lambda b, i: (b, i, 0, 0)),
            pl.BlockSpec((None, N_IDX_HEADS, TM, LANE), lambda b, i: (b, 0, i, 0)),
            pl.BlockSpec((None, TM, IDX_DIM), tok),
            pl.BlockSpec((None, TM, LANE), tok),
            pl.BlockSpec((None, N_IDX_HEADS, TM), lambda b, i: (b, 0, i)),
        ],
        out_shape=[
            jax.ShapeDtypeStruct((bsz, t, Q_COLS), BF16),
            kv_shape, kv_shape,
            jax.ShapeDtypeStruct((bsz, t, KV_COLS), BF16),
            jax.ShapeDtypeStruct((bsz, t // TK, N_KV_HEADS * VT_ROWS, TK), BF16),
            jax.ShapeDtypeStruct((bsz, N_IDX_HEADS, t, LANE), BF16),
            jax.ShapeDtypeStruct((bsz, t, IDX_DIM), F32),
            jax.ShapeDtypeStruct((bsz, t, LANE), BF16),
            jax.ShapeDtypeStruct((bsz, N_IDX_HEADS, t), F32),
        ],
        compiler_params=_params(("arbitrary", "arbitrary")),
        name="dsa_proj_prompt",
    )(x, *_arrays(resident), *carried)


def _proj_sample_call(x, mod_s, w_proj):
    n = x.shape[0]
    args = (x, mod_s, w_proj)
    kv_shape = (N_KV_HEADS * n, HEAD_DIM)
    shapes = ((n, Q_COLS), kv_shape, kv_shape, (n, QI_PAD), (n, IDX_DIM), (n, LANE), (n, LANE))
    return pl.pallas_call(
        _proj_sample_kernel,
        grid=(1,),
        in_specs=[_resident_spec(a) for a in args],
        out_specs=[_full_spec(s) for s in shapes],
        out_shape=[jax.ShapeDtypeStruct(s, F32) for s in shapes],
        compiler_params=_params(("arbitrary",)),
        name="dsa_proj_sample",
    )(*_arrays(args))


def _attend_prompt_kernel(x_ref, mod_ref, lng_ref, lnb_ref, q_ref, qi_ref, wit_ref, kb_ref, vt_ref, kib_ref,
                          w_out_ref, o_ref, sc_ref, sc16_ref, q4_ref, acc_ref):
    i = pl.program_id(1)
    n_chunks = (i * QB) // TK + 1
    sub_tiles = TK // SUBLANE

    for g in range(N_KV_HEADS):
        for hh in range(KV_GROUP):
            head = g * KV_GROUP + hh
            q4_ref[g, hh * QB:(hh + 1) * QB, :] = q_ref[:, head * HEAD_DIM:(head + 1) * HEAD_DIM]

    k_pos = lax.broadcasted_iota(I32, (TK, QB), 0)
    q_pos = i * QB + lax.broadcasted_iota(I32, (TK, QB), 1)

    def score_body(c, carry):
        start = pl.multiple_of(c * TK, TK)
        kslab = kib_ref[pl.ds(start, TK), :]
        sidx = jnp.zeros((TK, QB), F32)
        for pair in range(N_IDX_HEADS // 2):
            s = _dot_nt(kslab, qi_ref[2 * pair:2 * pair + 2].reshape(2 * QB, LANE))
            for j in range(2):
                hh = 2 * pair + j
                sidx = sidx + jnp.maximum(s[:, j * QB:(j + 1) * QB], 0.0) * wit_ref[hh:hh + 1, :]
        sidx = jnp.where(k_pos + start <= q_pos, sidx, jnp.nan)
        sc_ref[c] = sidx
        sc16_ref[c] = sidx.astype(BF16)
        return carry

    lax.fori_loop(0, n_chunks, score_body, 0)

    def count_ge_bf16(cand):
        pack = 2 * SUBLANE
        cand16 = jnp.concatenate([cand, cand], axis=0).astype(BF16)

        def body(c, accs):
            kc = sc16_ref[c]
            accs = list(accs)
            for t in range(TK // pack):
                hit = jnp.where(kc[t * pack:(t + 1) * pack, :] >= cand16, jnp.ones((), BF16), jnp.zeros((), BF16))
                accs[t % len(accs)] = accs[t % len(accs)] + hit
            return tuple(accs)

        zero = jnp.zeros((pack, QB), BF16)
        a0, a1, a2, a3 = lax.fori_loop(0, n_chunks, body, (zero,) * 4)
        return jnp.sum(((a0 + a1) + (a2 + a3)).astype(F32), axis=0, keepdims=True)

    def count_ge(cand):
        def body(c, accs):
            kc = sc_ref[c]
            accs = list(accs)
            for t in range(sub_tiles):
                hit = jnp.where(kc[t * SUBLANE:(t + 1) * SUBLANE, :] >= cand, 1.0, 0.0)
                accs[t % len(accs)] = accs[t % len(accs)] + hit
            return tuple(accs)

        zero = jnp.zeros((SUBLANE, QB), F32)
        a0, a1, a2, a3 = lax.fori_loop(0, n_chunks, body, (zero,) * 4)
        return jnp.sum((a0 + a1) + (a2 + a3), axis=0, keepdims=True)

    assert sc_ref.shape[0] * TK // (2 * SUBLANE) <= 256
    thr, excess = _kth_threshold(count_ge_bf16, count_ge, (SUBLANE, QB), TOPK_MAX)
    thr_full = jnp.concatenate([thr] * sub_tiles, axis=0)

    @pl.when(jnp.max(excess) > 0.0)
    def _():
        blk = TIE_BLOCK
        later_or_same = (lax.broadcasted_iota(I32, (blk, blk), 1) >= lax.broadcasted_iota(I32, (blk, blk), 0)).astype(BF16)

        def drop_body(carry):
            j, seen, _ = carry
            c = n_chunks - 1 - j
            blocks = [slice(b * blk, (b + 1) * blk) for b in range(TK // blk)]
            kc = [sc_ref[c, rows, :] for rows in blocks]
            tied = [k == thr_full[0:blk, :] for k in kc]
            local = [_dot(later_or_same, jnp.where(t, 1.0, 0.0).astype(BF16)) for t in tied]
            for b in reversed(range(len(blocks))):
                rank_from_end = local[b] + seen
                sc_ref[c, blocks[b], :] = jnp.where(tied[b] & (rank_from_end <= excess[0:1, :]), jnp.nan, kc[b])
                seen = rank_from_end[0:1, :]
            return j + 1, seen, jnp.max(excess[0:1, :] - seen) > 0.0

        lax.while_loop(lambda carry: (carry[0] < n_chunks) & carry[2], drop_body,
                       (jnp.int32(0), jnp.zeros((1, QB), F32), jnp.bool_(True)))

    acc_ref[...] = jnp.zeros(acc_ref.shape, F32)

    def attend_chunks(chunks, m):
        s = []
        for c in chunks:
            start = pl.multiple_of(c * TK, TK)
            bias = jnp.where(sc_ref[c] >= thr_full, 0.0, NEG)
            bias4 = jnp.concatenate([bias] * KV_GROUP, axis=1)
            s.append([_dot_nt(kb_ref[pl.ds(start, TK), g * HEAD_DIM:(g + 1) * HEAD_DIM], q4_ref[g]) + bias4
                      for g in range(N_KV_HEADS)])
        m = list(m)
        for c, s_c in zip(chunks, s):
            for g in range(N_KV_HEADS):
                m_new = jnp.maximum(m[g], jnp.max(s_c[g], axis=0, keepdims=True))
                p = jnp.exp2(s_c[g] - m_new).astype(BF16)
                pv = _dot(vt_ref[c, g * VT_ROWS:(g + 1) * VT_ROWS, :], p)
                acc_ref[g] = jnp.exp2(m[g] - m_new) * acc_ref[g] + pv
                m[g] = m_new
        return tuple(m)

    m = (jnp.full((1, KV_GROUP * QB), NEG, F32),) * N_KV_HEADS
    m = lax.fori_loop(0, n_chunks // 2, lambda j, m: attend_chunks((2 * j, 2 * j + 1), m), m)
    lax.fori_loop(0, n_chunks % 2, lambda _, m: attend_chunks((n_chunks - 1,), m), m)

    heads = []
    for g in range(N_KV_HEADS):
        og = acc_ref[g, 0:HEAD_DIM, :] / acc_ref[g, HEAD_DIM:HEAD_DIM + 1, :]
        heads += [og[:, hh * QB:(hh + 1) * QB].T for hh in range(KV_GROUP)]
    o = jnp.concatenate(heads, axis=1).astype(BF16)
    y = _dot(o, w_out_ref[...])
    o_ref[...] = _post_norm(x_ref[...], y, _prompt_mod(mod_ref, 2), lng_ref[...], lnb_ref[...])


def _attend_prompt_call(x, mod_p, lng, lnb, q, qi_hm, wit, kb, vt, kib, w_out):
    bsz, t, _ = x.shape
    tok = lambda b, i: (b, i, 0)
    seq = lambda b, i: (b, 0, 0)
    return pl.pallas_call(
        _attend_prompt_kernel,
        grid=(bsz, t // QB),
        in_specs=[
            pl.BlockSpec((None, QB, D_MODEL), tok),
            _resident_spec(mod_p), _const_spec((1, D_MODEL)), _const_spec((1, D_MODEL)),
            pl.BlockSpec((None, QB, Q_COLS), tok),
            pl.BlockSpec((None, N_IDX_HEADS, QB, LANE), lambda b, i: (b, 0, i, 0)),
            pl.BlockSpec((None, N_IDX_HEADS, QB), lambda b, i: (b, 0, i)),
            pl.BlockSpec((None, t, KV_COLS), seq),
            pl.BlockSpec((None, t // TK, N_KV_HEADS * VT_ROWS, TK), lambda b, i: (b, 0, 0, 0)),
            pl.BlockSpec((None, t, LANE), seq),
            _resident_spec(w_out),
        ],
        out_specs=pl.BlockSpec((None, QB, D_MODEL), tok),
        out_shape=jax.ShapeDtypeStruct(x.shape, F32),
        scratch_shapes=[
            pltpu.VMEM((t // TK, TK, QB), F32),
            pltpu.VMEM((t // TK, TK, QB), BF16),
            pltpu.VMEM((N_KV_HEADS, KV_GROUP * QB, HEAD_DIM), BF16),
            pltpu.VMEM((N_KV_HEADS, VT_ROWS, KV_GROUP * QB), F32),
        ],
        compiler_params=_params(("arbitrary", "arbitrary")),
        name="dsa_attend_prompt",
    )(x, *_arrays((mod_p, _row(lng), _row(lnb), q, qi_hm, wit, kb, vt, kib, w_out)))


def _page_copies(pt_ref, cache_ref, layer, buf_ref, sem_ref, slot, first_sample, n_samples, n_pages, along_lanes=False):
    rows, cols = cache_ref.shape[2:]
    copies = []
    for bb in range(n_samples):
        for p in range(n_pages):
            src = cache_ref.at[layer, pt_ref[first_sample + bb, p]]
            if along_lanes:
                dst = buf_ref.at[slot, bb, :, pl.ds(p * cols, cols)]
            else:
                dst = buf_ref.at[slot, bb, pl.ds(p * rows, rows), :]
            copies.append(pltpu.make_async_copy(src, dst, sem_ref.at[slot]))
    return copies


def _select_sample_kernel(pt_ref, qi_ref, wbc_ref, sn_ref, cache_ref, bias_ref, buf_ref, sc_ref, sem_ref,
                          *, layer, n_pages, page):
    step = pl.program_id(0)
    slot = step % 2
    past = n_pages * page
    width = past + LANE

    def copies(s, sl):
        return _page_copies(pt_ref, cache_ref, layer, buf_ref, sem_ref, sl, s * SB, SB, n_pages, along_lanes=True)

    @pl.when(step == 0)
    def _():
        for cp in copies(0, 0):
            cp.start()

    @pl.when(step + 1 < pl.num_programs(0))
    def _():
        for cp in copies(step + 1, 1 - slot):
            cp.start()

    for cp in copies(step, slot):
        cp.wait()

    for bb in range(SB):
        qs = qi_ref[bb][:, :IDX_DIM].astype(BF16)
        s = _dot(qs, buf_ref[slot, bb].astype(BF16))
        w = jnp.concatenate([wbc_ref[bb]] * (past // LANE), axis=1)
        sidx = jnp.sum(jnp.maximum(s, 0.0) * w, axis=0, keepdims=True)
        sc_ref[bb:bb + 1, 0:past] = sidx
    lane = lax.broadcasted_iota(I32, (SB, LANE), 1)
    sc_ref[:, past:width] = jnp.where(lane == 0, sn_ref[...], jnp.nan)

    lane_tiles = width // LANE
    sc = sc_ref[...]

    def count_ge(cand):
        acc = jnp.zeros((SB, LANE), F32)
        for t in range(lane_tiles):
            acc = acc + jnp.where(sc[:, t * LANE:(t + 1) * LANE] >= cand, 1.0, 0.0)
        return jnp.sum(acc, axis=1, keepdims=True)

    thr, excess = _kth_threshold(count_ge, count_ge, (SB, LANE), TOPK_MAX)
    thr_full = jnp.concatenate([thr] * lane_tiles, axis=1)
    bias_ref[...] = jnp.where(sc >= thr_full, 0.0, NEG)

    @pl.when(jnp.max(excess) > 0.0)
    def _():
        later_or_same = (lax.broadcasted_iota(I32, (LANE, LANE), 0) >= lax.broadcasted_iota(I32, (LANE, LANE), 1)).astype(BF16)
        seen = jnp.zeros((SB, 1), F32)
        for t in reversed(range(lane_tiles)):
            cols = slice(t * LANE, (t + 1) * LANE)
            tied = sc[:, cols] == thr
            rank_from_end = _dot(jnp.where(tied, 1.0, 0.0).astype(BF16), later_or_same) + seen
            bias_ref[:, cols] = jnp.where(tied & (rank_from_end <= excess), NEG, bias_ref[:, cols])
            seen = rank_from_end[:, 0:1]


def _select_sample_call(page_table, qi3, wbc, sn, cache_kidx_t, layer):
    n, n_pages = page_table.shape
    page = cache_kidx_t.shape[3]
    width = n_pages * page + LANE
    kern = functools.partial(_select_sample_kernel, layer=layer, n_pages=n_pages, page=page)
    return pl.pallas_call(
        kern,
        grid_spec=pltpu.PrefetchScalarGridSpec(
            num_scalar_prefetch=1,
            grid=(n // SB,),
            in_specs=[
                pl.BlockSpec((SB, N_IDX_HEADS, LANE), lambda s, pt: (s, 0, 0)),
                pl.BlockSpec((SB, N_IDX_HEADS, LANE), lambda s, pt: (s, 0, 0)),
                pl.BlockSpec((SB, LANE), lambda s, pt: (s, 0)),
                pl.BlockSpec(memory_space=pl.ANY),
            ],
            out_specs=pl.BlockSpec((SB, width), lambda s, pt: (s, 0)),
            scratch_shapes=[
                pltpu.VMEM((2, SB, IDX_DIM, n_pages * page), F32),
                pltpu.VMEM((SB, width), F32),
                pltpu.SemaphoreType.DMA((2,)),
            ],
        ),
        out_shape=jax.ShapeDtypeStruct((n, width), F32),
        compiler_params=_params(("arbitrary",)),
        name="dsa_select_sample",
    )(page_table, qi3, wbc, sn, cache_kidx_t)


def _attend_sample_kernel(pt_ref, q_ref, kn_ref, vn_ref, bias_ref, ck_ref, cv_ref, o_ref,
                          kbuf_ref, vbuf_ref, ksem_ref, vsem_ref, *, layer, n_pages, page):
    b = pl.program_id(0)
    slot = b % 2
    past = n_pages * page

    def copies(sample, sl):
        return (_page_copies(pt_ref, ck_ref, layer, kbuf_ref, ksem_ref, sl, sample, 1, n_pages)
                + _page_copies(pt_ref, cv_ref, layer, vbuf_ref, vsem_ref, sl, sample, 1, n_pages))

    @pl.when(b == 0)
    def _():
        for cp in copies(0, 0):
            cp.start()

    @pl.when(b + 1 < pl.num_programs(0))
    def _():
        for cp in copies(b + 1, 1 - slot):
            cp.start()

    for cp in copies(b, slot):
        cp.wait()

    qb = q_ref[...].astype(BF16)
    qr = qb.astype(F32)
    bias = bias_ref[...]
    head = lax.broadcasted_iota(I32, (N_HEADS, HEAD_DIM), 0)
    rows = [pl.ds(g, past, stride=N_KV_HEADS) for g in range(N_KV_HEADS)]
    s = [_dot_nt(qb, kbuf_ref[slot, 0, rows[g], :].astype(BF16)) + bias[:, :past] for g in range(N_KV_HEADS)]
    p, p_new, denom = [], [], []
    for g in range(N_KV_HEADS):
        k_new = kn_ref[g:g + 1, :].astype(BF16).astype(F32)
        s_new = jnp.sum(qr * k_new, axis=1, keepdims=True) + bias[:, past:past + 1]
        m = jnp.maximum(jnp.max(s[g], axis=1, keepdims=True), s_new)
        p.append(jnp.exp(s[g] - m))
        p_new.append(jnp.exp(s_new - m))
        denom.append(jnp.sum(p[g], axis=1, keepdims=True) + p_new[g])
    og = [_dot(p[g].astype(BF16), vbuf_ref[slot, 0, rows[g], :].astype(BF16)) for g in range(N_KV_HEADS)]
    o = jnp.zeros((N_HEADS, HEAD_DIM), F32)
    for g in range(N_KV_HEADS):
        v_new = vn_ref[g:g + 1, :].astype(BF16).astype(F32)
        o_g = (og[g] + p_new[g].astype(BF16).astype(F32) * v_new) / denom[g]
        o = jnp.where(head // KV_GROUP == g, o_g, o)
    o_ref[...] = o


def _attend_sample_call(page_table, q3, kn3, vn3, bias3, cache_k, cache_v, layer):
    n, n_pages = page_table.shape
    page = cache_k.shape[2] // N_KV_HEADS
    past = n_pages * page
    kern = functools.partial(_attend_sample_kernel, layer=layer, n_pages=n_pages, page=page)
    per = lambda b, pt: (b, 0, 0)
    return pl.pallas_call(
        kern,
        grid_spec=pltpu.PrefetchScalarGridSpec(
            num_scalar_prefetch=1,
            grid=(n,),
            in_specs=[
                pl.BlockSpec((None, N_HEADS, HEAD_DIM), per),
                pl.BlockSpec((None, N_KV_HEADS, HEAD_DIM), per),
                pl.BlockSpec((None, N_KV_HEADS, HEAD_DIM), per),
                pl.BlockSpec((None, 1, past + LANE), per),
                pl.BlockSpec(memory_space=pl.ANY),
                pl.BlockSpec(memory_space=pl.ANY),
            ],
            out_specs=pl.BlockSpec((None, N_HEADS, HEAD_DIM), per),
            scratch_shapes=[
                pltpu.VMEM((2, 1, N_KV_HEADS * past, HEAD_DIM), F32),
                pltpu.VMEM((2, 1, N_KV_HEADS * past, HEAD_DIM), F32),
                pltpu.SemaphoreType.DMA((2,)),
                pltpu.SemaphoreType.DMA((2,)),
            ],
        ),
        out_shape=jax.ShapeDtypeStruct((n, N_HEADS, HEAD_DIM), F32),
        compiler_params=_params(("arbitrary",)),
        name="dsa_attend_sample",
    )(page_table, q3, kn3, vn3, bias3, cache_k, cache_v)


def _out_sample_kernel(x_ref, mod_ref, lng_ref, lnb_ref, o_ref, w_out_ref, y_ref):
    y = _dot(o_ref[...].astype(BF16), w_out_ref[...])
    y_ref[...] = _post_norm(x_ref[...], y, mod_ref[2], lng_ref[...], lnb_ref[...])


def _out_sample_call(x, mod_s, lng, lnb, o, w_out):
    args = (x, mod_s, _row(lng), _row(lnb), o, w_out)
    return pl.pallas_call(
        _out_sample_kernel,
        grid=(1,),
        in_specs=[_resident_spec(a) for a in args],
        out_specs=_full_spec(x.shape),
        out_shape=jax.ShapeDtypeStruct(x.shape, F32),
        compiler_params=_params(("arbitrary",)),
        name="dsa_out_sample",
    )(*_arrays(args))


def _pack_proj_weight(w_in):
    d = w_in.shape[0]
    o_qi = Q_COLS + 2 * KV_COLS
    o_ki = o_qi + QI_COLS
    o_wi = o_ki + IDX_DIM
    qi = w_in[:, o_qi:o_ki].reshape(d, N_IDX_HEADS, IDX_DIM)
    qi = jnp.pad(qi, ((0, 0), (0, 0), (0, LANE - IDX_DIM))).reshape(d, QI_PAD)
    ki = jnp.pad(w_in[:, o_ki:o_wi], ((0, 0), (0, LANE - IDX_DIM)))
    wi = jnp.pad(w_in[:, o_wi:], ((0, 0), (0, LANE - N_IDX_HEADS)))
    return jnp.concatenate([w_in[:, :o_qi], qi, ki, wi], axis=1).astype(BF16)


def _pack_proj_weight_t(w_in):
    o_v = Q_COLS + KV_COLS
    o_wi = Q_COLS + 2 * KV_COLS + QI_COLS + IDX_DIM
    wt = jnp.concatenate([w_in[:, o_v:o_v + KV_COLS], w_in[:, o_wi:]], axis=1).T
    return jnp.pad(wt, ((0, 2 * SUBLANE - N_IDX_HEADS), (0, 0))).astype(BF16)


def kernel(x_prompt, x_sample, cache_k, cache_v, cache_kidx, state_conv, page_table, c_prompt, c_sample,
           w_ada, b_ada, ln_g, ln_b, sgu_w_in, sgu_b_in, sgu_norm_g, sgu_norm_b, sgu_w_s, sgu_b_s, sgu_w_out,
           dsa_w_in, dsa_w_out, ffn_w_up, ffn_conv_w, ffn_conv_b, ffn_w_down):
    bsz, t_p, _ = x_prompt.shape
    n_s = x_sample.shape[0]
    n_phys, page = cache_k.shape[1], cache_k.shape[2]
    past = page_table.shape[1] * page

    c_prompt8 = jnp.pad(c_prompt, ((0, SUBLANE - bsz), (0, 0)))
    mods_s, mods_p = _ada_call(c_sample, c_prompt8, w_ada, b_ada)

    ck = cache_k.reshape(cache_k.shape[0], n_phys, page * N_KV_HEADS, HEAD_DIM)
    cv = cache_v.reshape(cache_v.shape[0], n_phys, page * N_KV_HEADS, HEAD_DIM)
    ckidx_t = jnp.swapaxes(cache_kidx, 2, 3)

    sgu_w_in_b, sgu_w_out_b = sgu_w_in.astype(BF16), sgu_w_out.astype(BF16)
    dsa_w_out_b = dsa_w_out.astype(BF16)
    ffn_w_up_b, ffn_w_down_b = ffn_w_up.astype(BF16), ffn_w_down.astype(BF16)
    n_dsa = DEPTH // 2

    xp = x_prompt
    xs = x_sample.reshape(n_s, D_MODEL)
    kv_stacks = tuple(jnp.zeros((n_dsa, bsz, N_KV_HEADS * t_p, HEAD_DIM), F32) for _ in range(2))
    kip_l, ks_l, vs_l, kis_l, sgu_l, convp_l, convs_l = [], [], [], [], [], [], []
    for i in range(DEPTH):
        j = i // 2
        mod_p, mod_s = (mods_p, i), (mods_s, i)
        if i % 2 == 0:
            w_in = (sgu_w_in_b, j)
            w_out = (sgu_w_out_b, j)
            wtril = jnp.tril(sgu_w_s[j]).astype(BF16)
            bs_full = jnp.repeat(sgu_b_s[j].T, SGU_GROUP_DIM, axis=1)
            ws0 = jnp.repeat(sgu_w_s[j][:, 0, 0], SGU_GROUP_DIM)
            bs0 = jnp.repeat(sgu_b_s[j][:, 0], SGU_GROUP_DIM)
            xp = _sgu_prompt_call(xp, mod_p, ln_g[i, 0], ln_b[i, 0], w_in, sgu_b_in[j], sgu_norm_g[j],
                                  sgu_norm_b[j], wtril, bs_full, w_out)
            xs, v_rows = _sgu_sample_call(xs, mod_s, ln_g[i, 0], ln_b[i, 0], w_in, sgu_b_in[j], sgu_norm_g[j],
                                          sgu_norm_b[j], ws0, bs0, w_out)
            sgu_l.append(v_rows.reshape(n_s, 1, D_SGU))
        else:
            w_proj = _pack_proj_weight(dsa_w_in[j])
            w_out = (dsa_w_out_b, j)
            q, k_stack, v_stack, kb, vt, qi_hm, ki, kib, wit = _proj_prompt_call(
                xp, mod_p, w_proj, _pack_proj_weight_t(dsa_w_in[j]), j, n_dsa, kv_stacks)
            kv_stacks = (k_stack, v_stack)
            xp = _attend_prompt_call(xp, mod_p, ln_g[i, 0], ln_b[i, 0], q, qi_hm, wit, kb, vt, kib, w_out)
            kip_l.append(ki)

            qs, ks_new, vs_new, qis, kis_new, wis, sn = _proj_sample_call(xs, mod_s, w_proj)
            wbc = jnp.broadcast_to(wis[:, :N_IDX_HEADS, None], (n_s, N_IDX_HEADS, LANE))
            bias = _select_sample_call(page_table, qis.reshape(n_s, N_IDX_HEADS, LANE), wbc, sn, ckidx_t, j)
            o = _attend_sample_call(page_table, qs.reshape(n_s, N_HEADS, HEAD_DIM),
                                    ks_new.reshape(n_s, N_KV_HEADS, HEAD_DIM),
                                    vs_new.reshape(n_s, N_KV_HEADS, HEAD_DIM),
                                    bias.reshape(n_s, 1, past + LANE), ck, cv, j)
            xs = _out_sample_call(xs, mod_s, ln_g[i, 0], ln_b[i, 0], o.reshape(n_s, Q_COLS), w_out)
            ks_l.append(ks_new.reshape(n_s, 1, N_KV_HEADS, HEAD_DIM))
            vs_l.append(vs_new.reshape(n_s, 1, N_KV_HEADS, HEAD_DIM))
            kis_l.append(kis_new.reshape(n_s, 1, IDX_DIM))

        w_up = (ffn_w_up_b, i)
        w_down = (ffn_w_down_b, i)
        xp, conv_p = _ffn_prompt_call(xp, mod_p, ln_g[i, 1], ln_b[i, 1], w_up, ffn_conv_w[i], ffn_conv_b[i], w_down)
        past_t = jnp.swapaxes(state_conv[i], 0, 1)
        xs, a_s = _ffn_sample_call(xs, mod_s, ln_g[i, 1], ln_b[i, 1], w_up, ffn_conv_w[i], ffn_conv_b[i], w_down,
                                   past_t)
        convp_l.append(conv_p)
        convs_l.append(jnp.stack([state_conv[i][:, 1], a_s], axis=1))

    return (xp, xs.reshape(n_s, 1, D_MODEL),
            kv_stacks[0].reshape(n_dsa, bsz, t_p, N_KV_HEADS, HEAD_DIM),
            kv_stacks[1].reshape(n_dsa, bsz, t_p, N_KV_HEADS, HEAD_DIM), jnp.stack(kip_l),
            jnp.stack(ks_l), jnp.stack(vs_l), jnp.stack(kis_l),
            jnp.stack(sgu_l), jnp.stack(convp_l), jnp.stack(convs_l))
```

```python
import functools

import jax
import jax.numpy as jnp
from jax import lax
from jax.experimental import pallas as pl
from jax.experimental.pallas import tpu as pltpu

F32 = jnp.float32
BF16 = jnp.bfloat16
I32 = jnp.int32

D_MODEL = 1024
DEPTH = 4
N_MOD = 6
CHUNK = 128
D_SGU = D_MODEL
SGU_GROUPS = 8
SGU_GROUP_DIM = D_SGU // SGU_GROUPS
N_HEADS = 8
HEAD_DIM = D_MODEL // N_HEADS
N_KV_HEADS = 2
KV_GROUP = N_HEADS // N_KV_HEADS
N_IDX_HEADS = 8
IDX_DIM = 64
TOPK_MAX = 256
Q_COLS = N_HEADS * HEAD_DIM
KV_COLS = N_KV_HEADS * HEAD_DIM
QI_COLS = N_IDX_HEADS * IDX_DIM
D_FF = 2816
CONV_W = 3
ALPHA = (2 * DEPTH) ** 0.25
LN_EPS = 1e-5

LANE = 128
SUBLANE = 8
VMEM_LIMIT = 56 * 1024 * 1024

TM = 512
FF_CHUNK = 256
N_FF_CHUNKS = D_FF // FF_CHUNK
QB = 128
TK = 512
TIE_BLOCK = 256
KEY_MIN = -2 ** 31
NEG = -1e30
QI_PAD = N_IDX_HEADS * LANE
PROJ_COLS = Q_COLS + 2 * KV_COLS + QI_PAD + 2 * LANE
SB = 8
LOG2E = 1.4426950408889634
VT_ROWS = HEAD_DIM + 16


def _dot(a, b):
    return jnp.dot(a, b, preferred_element_type=F32)


def _dot_nt(a, b):
    return lax.dot_general(a, b, (((1,), (1,)), ((), ())), preferred_element_type=F32)


def _ln(x):
    mu = jnp.mean(x, axis=-1, keepdims=True)
    xc = x - mu
    var = jnp.mean(xc * xc, axis=-1, keepdims=True)
    return xc * lax.rsqrt(var + LN_EPS)


def _modulate(x, shift, scale):
    return x * (1.0 + scale) + shift


def _post_norm(x, y, gate, g, b):
    return _ln(ALPHA * x + (1.0 + gate) * y) * g + b


def _prompt_mod(mod_ref, m):
    return mod_ref[m, pl.ds(pl.program_id(0), 1), :]


def _key_value(key):
    bits = jnp.where(key < 0, jnp.int32(KEY_MIN) - key, key)
    return lax.bitcast_convert_type(bits, F32)


def _kth_threshold(count_ge, shape, kth):
    def bit_body(it, carry):
        key, cnt = carry
        cand = key + jnp.left_shift(jnp.int32(1), 31 - it)
        c = jnp.broadcast_to(count_ge(_key_value(cand)), shape)
        ok = c >= kth
        return jnp.where(ok, cand, key), jnp.where(ok, c, cnt)

    key, cnt = lax.fori_loop(0, 32, bit_body, (jnp.full(shape, KEY_MIN, I32), jnp.full(shape, kth, F32)))
    return jnp.where(key == KEY_MIN, -jnp.inf, _key_value(key)), cnt - kth


def _params(sem=None):
    return pltpu.CompilerParams(dimension_semantics=sem, vmem_limit_bytes=VMEM_LIMIT)


def _const_spec(shape):
    return pl.BlockSpec(shape, lambda *_: (0,) * len(shape), pipeline_mode=pl.Buffered(1))


def _full_spec(shape):
    return pl.BlockSpec(shape, lambda *_: (0,) * len(shape))


def _resident_spec(a):
    if isinstance(a, tuple):
        arr, layer = a
        tail = (0,) * (arr.ndim - 1)
        return pl.BlockSpec((None,) + arr.shape[1:], lambda *_: (layer,) + tail, pipeline_mode=pl.Buffered(1))
    return _const_spec(a.shape)


def _arrays(args):
    return [a[0] if isinstance(a, tuple) else a for a in args]


def _ada_kernel(cs_ref, cp_ref, w_ref, b_ref, os_ref, op_ref):
    w = w_ref[...].astype(BF16)
    bias = b_ref[...]
    os_ref[...] = _dot(jax.nn.silu(cs_ref[...]).astype(BF16), w) + bias
    op_ref[...] = _dot(jax.nn.silu(cp_ref[...]).astype(BF16), w) + bias


def _ada_call(c_sample, c_prompt8, w_ada, b_ada):
    tn = 512
    nn = D_MODEL // tn
    n_s = c_sample.shape[0]
    return pl.pallas_call(
        _ada_kernel,
        grid=(DEPTH, N_MOD, nn),
        in_specs=[
            pl.BlockSpec((n_s, D_MODEL), lambda l, m, n: (0, 0)),
            pl.BlockSpec((SUBLANE, D_MODEL), lambda l, m, n: (0, 0)),
            pl.BlockSpec((None, D_MODEL, tn), lambda l, m, n: (l, 0, m * nn + n)),
            pl.BlockSpec((None, 1, tn), lambda l, m, n: (l, 0, m * nn + n)),
        ],
        out_specs=[
            pl.BlockSpec((None, None, n_s, tn), lambda l, m, n: (l, m, 0, n)),
            pl.BlockSpec((None, None, SUBLANE, tn), lambda l, m, n: (l, m, 0, n)),
        ],
        out_shape=[
            jax.ShapeDtypeStruct((DEPTH, N_MOD, n_s, D_MODEL), F32),
            jax.ShapeDtypeStruct((DEPTH, N_MOD, SUBLANE, D_MODEL), F32),
        ],
        compiler_params=_params(("arbitrary",) * 3),
        name="ada_params",
    )(c_sample, c_prompt8, w_ada, b_ada.reshape(DEPTH, 1, N_MOD * D_MODEL))


def _sgu_front(x, shift, scale, w_in_ref, b_in_ref, ng_ref, nb_ref):
    h = _modulate(x, shift, scale).astype(BF16)
    u = jax.nn.gelu(_dot(h, w_in_ref[:, :D_SGU]) + b_in_ref[:, :D_SGU])
    v = jax.nn.gelu(_dot(h, w_in_ref[:, D_SGU:]) + b_in_ref[:, D_SGU:])
    v = _ln(v) * ng_ref[...] + nb_ref[...]
    return u, v


def _sgu_prompt_kernel(x_ref, mod_ref, lng_ref, lnb_ref, w_in_ref, b_in_ref, ng_ref, nb_ref,
                       wtril_ref, bs_ref, w_out_ref, o_ref, gated_ref):
    x = x_ref[...]
    u, v = _sgu_front(x, _prompt_mod(mod_ref, 0), _prompt_mod(mod_ref, 1),
                      w_in_ref, b_in_ref, ng_ref, nb_ref)
    vb = v.astype(BF16)
    n_chunks = x.shape[0] // CHUNK
    for g in range(SGU_GROUPS):
        cols = slice(g * SGU_GROUP_DIM, (g + 1) * SGU_GROUP_DIM)
        rhs = jnp.concatenate([vb[n * CHUNK:(n + 1) * CHUNK, cols] for n in range(n_chunks)], axis=1)
        mixed = _dot(wtril_ref[g], rhs)
        for n in range(n_chunks):
            rows = slice(n * CHUNK, (n + 1) * CHUNK)
            mix_n = mixed[:, n * SGU_GROUP_DIM:(n + 1) * SGU_GROUP_DIM] + bs_ref[:, cols]
            gated_ref[rows, cols] = (u[rows, cols] * mix_n).astype(BF16)
    y = _dot(gated_ref[...], w_out_ref[...])
    o_ref[...] = _post_norm(x, y, _prompt_mod(mod_ref, 2), lng_ref[...], lnb_ref[...])


def _sgu_sample_kernel(x_ref, mod_ref, lng_ref, lnb_ref, w_in_ref, b_in_ref, ng_ref, nb_ref,
                       ws0_ref, bs0_ref, w_out_ref, o_ref, v_ref):
    x = x_ref[...]
    u, v = _sgu_front(x, mod_ref[0], mod_ref[1], w_in_ref, b_in_ref, ng_ref, nb_ref)
    v_ref[...] = v
    mixed = v * ws0_ref[...] + bs0_ref[...]
    y = _dot((u * mixed).astype(BF16), w_out_ref[...])
    o_ref[...] = _post_norm(x, y, mod_ref[2], lng_ref[...], lnb_ref[...])


def _row(v):
    return v.reshape(1, -1)


def _sgu_prompt_call(x, mod_p, lng, lnb, w_in, b_in, ng, nb, wtril, bs_full, w_out):
    bsz, t, _ = x.shape
    resident = (mod_p, _row(lng), _row(lnb), w_in, _row(b_in), _row(ng), _row(nb), wtril, bs_full, w_out)
    return pl.pallas_call(
        _sgu_prompt_kernel,
        grid=(bsz, t // TM),
        in_specs=[pl.BlockSpec((None, TM, D_MODEL), lambda b, i: (b, i, 0))] + [_resident_spec(a) for a in resident],
        out_specs=pl.BlockSpec((None, TM, D_MODEL), lambda b, i: (b, i, 0)),
        out_shape=jax.ShapeDtypeStruct(x.shape, F32),
        scratch_shapes=[pltpu.VMEM((TM, D_SGU), BF16)],
        compiler_params=_params(("arbitrary", "arbitrary")),
        name="sgu_prompt",
    )(x, *_arrays(resident))


def _sgu_sample_call(x, mod_s, lng, lnb, w_in, b_in, ng, nb, ws0, bs0, w_out):
    n = x.shape[0]
    args = (x, mod_s, _row(lng), _row(lnb), w_in, _row(b_in), _row(ng), _row(nb), _row(ws0), _row(bs0), w_out)
    return pl.pallas_call(
        _sgu_sample_kernel,
        grid=(1,),
        in_specs=[_resident_spec(a) for a in args],
        out_specs=[_full_spec((n, D_MODEL)), _full_spec((n, D_SGU))],
        out_shape=[jax.ShapeDtypeStruct((n, D_MODEL), F32), jax.ShapeDtypeStruct((n, D_SGU), F32)],
        compiler_params=_params(("arbitrary",)),
        name="sgu_sample",
    )(*_arrays(args))


def _ffn_prompt_kernel(x_ref, mod_ref, lng_ref, lnb_ref, w_up_ref, cw_ref, cb_ref, w_down_ref,
                       o_ref, st_ref, carry_ref, g_ref, h_ref):
    i = pl.program_id(1)
    tm = x_ref.shape[0]

    @pl.when(i == 0)
    def _():
        carry_ref[...] = jnp.zeros(carry_ref.shape, F32)

    h_ref[...] = _modulate(x_ref[...], _prompt_mod(mod_ref, 3), _prompt_mod(mod_ref, 4)).astype(BF16)

    def up(c):
        a = _dot(h_ref[...], w_up_ref[:, c * FF_CHUNK:(c + 1) * FF_CHUNK])
        u = _dot(h_ref[...], w_up_ref[:, D_FF + c * FF_CHUNK:D_FF + (c + 1) * FF_CHUNK])
        return a, u

    head_row = lax.broadcasted_iota(I32, (SUBLANE, FF_CHUNK), 0)
    nxt = up(0)
    for c in range(N_FF_CHUNKS):
        cols = slice(c * FF_CHUNK, (c + 1) * FF_CHUNK)
        a, u = nxt
        if c + 1 < N_FF_CHUNKS:
            nxt = up(c + 1)
        prev = carry_ref[c]
        shifted = []
        for j in range(1, CONV_W):
            rolled = pltpu.roll(a, j, axis=0)
            head = jnp.where(head_row < j, pltpu.roll(prev, j, axis=0), rolled[0:SUBLANE, :])
            shifted.append(jnp.concatenate([head, rolled[SUBLANE:, :]], axis=0))
        a_m1, a_m2 = shifted
        conv = a_m2 * cw_ref[0:1, cols] + a_m1 * cw_ref[1:2, cols] + a * cw_ref[2:3, cols] + cb_ref[:, cols]
        carry_ref[c] = a[tm - SUBLANE:tm, :]
        g_ref[:, cols] = (jax.nn.gelu(conv) * u).astype(BF16)

    @pl.when(i == pl.num_programs(1) - 1)
    def _():
        for c in range(N_FF_CHUNKS):
            st_ref[:, c * FF_CHUNK:(c + 1) * FF_CHUNK] = carry_ref[c, SUBLANE - (CONV_W - 1):SUBLANE, :]

    gate, lng, lnb = _prompt_mod(mod_ref, 5), lng_ref[...], lnb_ref[...]
    half = tm // 2
    for r in range(2):
        rows = slice(r * half, (r + 1) * half)
        y = _dot(g_ref[rows, :], w_down_ref[...])
        o_ref[rows, :] = _post_norm(x_ref[rows, :], y, gate, lng, lnb)


def _ffn_sample_kernel(x_ref, mod_ref, lng_ref, lnb_ref, w_up_ref, cw_ref, cb_ref, w_down_ref, past_ref,
                       o_ref, a_ref, g_ref):
    x = x_ref[...]
    h = _modulate(x, mod_ref[3], mod_ref[4]).astype(BF16)
    for c in range(N_FF_CHUNKS):
        cols = slice(c * FF_CHUNK, (c + 1) * FF_CHUNK)
        ucols = slice(D_FF + c * FF_CHUNK, D_FF + (c + 1) * FF_CHUNK)
        a = _dot(h, w_up_ref[:, cols])
        a_ref[:, cols] = a
        conv = (past_ref[0, :, cols] * cw_ref[0:1, cols] + past_ref[1, :, cols] * cw_ref[1:2, cols]
                + a * cw_ref[2:3, cols] + cb_ref[:, cols])
        u = _dot(h, w_up_ref[:, ucols])
        g_ref[:, cols] = (jax.nn.gelu(conv) * u).astype(BF16)
    y = _dot(g_ref[...], w_down_ref[...])
    o_ref[...] = _post_norm(x, y, mod_ref[5], lng_ref[...], lnb_ref[...])


def _ffn_prompt_call(x, mod_p, lng, lnb, w_up, cw, cb, w_down):
    bsz, t, _ = x.shape
    resident = (mod_p, _row(lng), _row(lnb), w_up, cw, _row(cb), w_down)
    return pl.pallas_call(
        _ffn_prompt_kernel,
        grid=(bsz, t // TM),
        in_specs=[pl.BlockSpec((None, TM, D_MODEL), lambda b, i: (b, i, 0))] + [_resident_spec(a) for a in resident],
        out_specs=[
            pl.BlockSpec((None, TM, D_MODEL), lambda b, i: (b, i, 0)),
            pl.BlockSpec((None, CONV_W - 1, D_FF), lambda b, i: (b, 0, 0)),
        ],
        out_shape=[jax.ShapeDtypeStruct(x.shape, F32), jax.ShapeDtypeStruct((bsz, CONV_W - 1, D_FF), F32)],
        scratch_shapes=[pltpu.VMEM((N_FF_CHUNKS, SUBLANE, FF_CHUNK), F32), pltpu.VMEM((TM, D_FF), BF16),
                        pltpu.VMEM((TM, D_MODEL), BF16)],
        compiler_params=_params(("arbitrary", "arbitrary")),
        name="ffn_prompt",
    )(x, *_arrays(resident))


def _ffn_sample_call(x, mod_s, lng, lnb, w_up, cw, cb, w_down, past_t):
    n = x.shape[0]
    args = (x, mod_s, _row(lng), _row(lnb), w_up, cw, _row(cb), w_down, past_t)
    return pl.pallas_call(
        _ffn_sample_kernel,
        grid=(1,),
        in_specs=[_resident_spec(a) for a in args],
        out_specs=[_full_spec((n, D_MODEL)), _full_spec((n, D_FF))],
        out_shape=[jax.ShapeDtypeStruct((n, D_MODEL), F32), jax.ShapeDtypeStruct((n, D_FF), F32)],
        scratch_shapes=[pltpu.VMEM((n, D_FF), BF16)],
        compiler_params=_params(("arbitrary",)),
        name="ffn_sample",
    )(*_arrays(args))


_C_K = Q_COLS
_C_V = _C_K + KV_COLS
_C_QI = _C_V + KV_COLS
_C_KI = _C_QI + QI_PAD
_C_WI = _C_KI + LANE


def _proj_common(h, w_ref):
    q = _dot(h, w_ref[:, 0:_C_K]) * HEAD_DIM ** -0.5
    k = _dot(h, w_ref[:, _C_K:_C_V])
    v = _dot(h, w_ref[:, _C_V:_C_QI])
    qi = _dot(h, w_ref[:, _C_QI:_C_KI])
    ki = _dot(h, w_ref[:, _C_KI:_C_WI])
    wi = _dot(h, w_ref[:, _C_WI:PROJ_COLS]) * N_IDX_HEADS ** -0.5 * IDX_DIM ** -0.5
    return q, k, v, qi, ki, wi


def _store_kv_rows(ref, x):
    n = x.shape[0]
    for g in range(N_KV_HEADS):
        ref[pl.ds(g, n, stride=N_KV_HEADS), :] = x[:, g * HEAD_DIM:(g + 1) * HEAD_DIM]


def _proj_prompt_kernel(x_ref, mod_ref, w_ref, wt_ref, *refs, n_carried):
    q_ref, k_ref, v_ref, kb_ref, vt_ref, qi_ref, ki_ref, kib_ref, wit_ref = refs[n_carried:]
    h = _modulate(x_ref[...], _prompt_mod(mod_ref, 0), _prompt_mod(mod_ref, 1)).astype(BF16)
    q, k, v, qi, ki, _ = _proj_common(h, w_ref)
    q_ref[...] = (q * LOG2E).astype(BF16)
    _store_kv_rows(k_ref, k)
    _store_kv_rows(v_ref, v)
    kb_ref[...] = k.astype(BF16)
    for hh in range(N_IDX_HEADS):
        qi_ref[hh] = qi[:, hh * LANE:(hh + 1) * LANE].astype(BF16)
    ki_ref[...] = ki[:, :IDX_DIM]
    kib_ref[...] = ki.astype(BF16)
    vt = _dot_nt(wt_ref[0:KV_COLS, :], h).astype(BF16)
    ones_rows = (lax.broadcasted_iota(I32, (VT_ROWS - HEAD_DIM, vt.shape[1]), 0) == 0).astype(BF16)
    for g in range(N_KV_HEADS):
        vt_ref[0, g * VT_ROWS:g * VT_ROWS + HEAD_DIM, :] = vt[g * HEAD_DIM:(g + 1) * HEAD_DIM, :]
        vt_ref[0, g * VT_ROWS + HEAD_DIM:(g + 1) * VT_ROWS, :] = ones_rows
    wit = _dot_nt(wt_ref[KV_COLS:, :], h) * N_IDX_HEADS ** -0.5 * IDX_DIM ** -0.5
    wit_ref[...] = wit[0:N_IDX_HEADS, :]


def _proj_sample_kernel(x_ref, mod_ref, w_ref, q_ref, k_ref, v_ref, qi_ref, ki_ref, wi_ref, sn_ref):
    h = _modulate(x_ref[...], mod_ref[0], mod_ref[1]).astype(BF16)
    q, k, v, qi, ki, wi = _proj_common(h, w_ref)
    q_ref[...] = q
    _store_kv_rows(k_ref, k)
    _store_kv_rows(v_ref, v)
    qi_ref[...] = qi
    ki_ref[...] = ki[:, :IDX_DIM]
    wi_ref[...] = wi
    kr = ki.astype(BF16).astype(F32)
    sn = jnp.zeros((x_ref.shape[0], 1), F32)
    for hh in range(N_IDX_HEADS):
        qr = qi[:, hh * LANE:(hh + 1) * LANE].astype(BF16).astype(F32)
        sh = jnp.sum(qr * kr, axis=1, keepdims=True)
        sn = sn + jnp.maximum(sh, 0.0) * wi[:, hh:hh + 1]
    sn_ref[...] = jnp.broadcast_to(sn, sn_ref.shape)


def _proj_prompt_call(x, mod_p, w_proj, w_proj_t, slot, n_slots, kv_stacks):
    bsz, t, _ = x.shape
    assert TM == TK
    tok = lambda b, i: (b, i, 0)
    resident = (mod_p, w_proj, w_proj_t)
    carried = tuple(kv_stacks)
    kv_spec = pl.BlockSpec((None, None, N_KV_HEADS * TM, HEAD_DIM), lambda b, i: (slot, b, i, 0))
    kv_shape = jax.ShapeDtypeStruct((n_slots, bsz, N_KV_HEADS * t, HEAD_DIM), F32)
    return pl.pallas_call(
        functools.partial(_proj_prompt_kernel, n_carried=len(carried)),
        grid=(bsz, t // TM),
        in_specs=([pl.BlockSpec((None, TM, D_MODEL), tok)] + [_resident_spec(a) for a in resident]
                  + [pl.BlockSpec(memory_space=pl.ANY)] * len(carried)),
        input_output_aliases={1 + len(resident) + n: 1 + n for n in range(len(carried))},
        out_specs=[
            pl.BlockSpec((None, TM, Q_COLS), tok),
            kv_spec, kv_spec,
            pl.BlockSpec((None, TM, KV_COLS), tok),
            pl.BlockSpec((None, 1, N_KV_HEADS * VT_ROWS, TK), lambda b, i: (b, i, 0, 0)),
            pl.BlockSpec((None, N_IDX_HEADS, TM, LANE), lambda b, i: (b, 0, i, 0)),
            pl.BlockSpec((None, TM, IDX_DIM), tok),
            pl.BlockSpec((None, TM, LANE), tok),
            pl.BlockSpec((None, N_IDX_HEADS, TM), lambda b, i: (b, 0, i)),
        ],
        out_shape=[
            jax.ShapeDtypeStruct((bsz, t, Q_COLS), BF16),
            kv_shape, kv_shape,
            jax.ShapeDtypeStruct((bsz, t, KV_COLS), BF16),
            jax.ShapeDtypeStruct((bsz, t // TK, N_KV_HEADS * VT_ROWS, TK), BF16),
            jax.ShapeDtypeStruct((bsz, N_IDX_HEADS, t, LANE), BF16),
            jax.ShapeDtypeStruct((bsz, t, IDX_DIM), F32),
            jax.ShapeDtypeStruct((bsz, t, LANE), BF16),
            jax.ShapeDtypeStruct((bsz, N_IDX_HEADS, t), F32),
        ],
        compiler_params=_params(("arbitrary", "arbitrary")),
        name="dsa_proj_prompt",
    )(x, *_arrays(resident), *carried)


def _proj_sample_call(x, mod_s, w_proj):
    n = x.shape[0]
    args = (x, mod_s, w_proj)
    kv_shape = (N_KV_HEADS * n, HEAD_DIM)
    shapes = ((n, Q_COLS), kv_shape, kv_shape, (n, QI_PAD), (n, IDX_DIM), (n, LANE), (n, LANE))
    return pl.pallas_call(
        _proj_sample_kernel,
        grid=(1,),
        in_specs=[_resident_spec(a) for a in args],
        out_specs=[_full_spec(s) for s in shapes],
        out_shape=[jax.ShapeDtypeStruct(s, F32) for s in shapes],
        compiler_params=_params(("arbitrary",)),
        name="dsa_proj_sample",
    )(*_arrays(args))


def _attend_prompt_kernel(x_ref, mod_ref, lng_ref, lnb_ref, q_ref, qi_ref, wit_ref, kb_ref, vt_ref, kib_ref,
                          w_out_ref, o_ref, sc_ref, q4_ref, acc_ref):
    i = pl.program_id(1)
    n_chunks = (i * QB) // TK + 1
    sub_tiles = TK // SUBLANE

    for g in range(N_KV_HEADS):
        for hh in range(KV_GROUP):
            head = g * KV_GROUP + hh
            q4_ref[g, hh * QB:(hh + 1) * QB, :] = q_ref[:, head * HEAD_DIM:(head + 1) * HEAD_DIM]

    k_pos = lax.broadcasted_iota(I32, (TK, QB), 0)
    q_pos = i * QB + lax.broadcasted_iota(I32, (TK, QB), 1)

    def score_body(c, carry):
        start = pl.multiple_of(c * TK, TK)
        kslab = kib_ref[pl.ds(start, TK), :]
        sidx = jnp.zeros((TK, QB), F32)
        for pair in range(N_IDX_HEADS // 2):
            s = _dot_nt(kslab, qi_ref[2 * pair:2 * pair + 2].reshape(2 * QB, LANE))
            for j in range(2):
                hh = 2 * pair + j
                sidx = sidx + jnp.maximum(s[:, j * QB:(j + 1) * QB], 0.0) * wit_ref[hh:hh + 1, :]
        sc_ref[c] = jnp.where(k_pos + start <= q_pos, sidx, jnp.nan)
        return carry

    lax.fori_loop(0, n_chunks, score_body, 0)

    def count_ge(cand):
        def body(c, accs):
            kc = sc_ref[c]
            accs = list(accs)
            for t in range(sub_tiles):
                hit = jnp.where(kc[t * SUBLANE:(t + 1) * SUBLANE, :] >= cand, 1.0, 0.0)
                accs[t % len(accs)] = accs[t % len(accs)] + hit
            return tuple(accs)

        zero = jnp.zeros((SUBLANE, QB), F32)
        a0, a1, a2, a3 = lax.fori_loop(0, n_chunks, body, (zero,) * 4)
        return jnp.sum((a0 + a1) + (a2 + a3), axis=0, keepdims=True)

    thr, excess = _kth_threshold(count_ge, (SUBLANE, QB), TOPK_MAX)
    thr_full = jnp.concatenate([thr] * sub_tiles, axis=0)

    @pl.when(jnp.max(excess) > 0.0)
    def _():
        blk = TIE_BLOCK
        later_or_same = (lax.broadcasted_iota(I32, (blk, blk), 1) >= lax.broadcasted_iota(I32, (blk, blk), 0)).astype(BF16)

        def drop_body(carry):
            j, seen, _ = carry
            c = n_chunks - 1 - j
            blocks = [slice(b * blk, (b + 1) * blk) for b in range(TK // blk)]
            kc = [sc_ref[c, rows, :] for rows in blocks]
            tied = [k == thr_full[0:blk, :] for k in kc]
            local = [_dot(later_or_same, jnp.where(t, 1.0, 0.0).astype(BF16)) for t in tied]
            for b in reversed(range(len(blocks))):
                rank_from_end = local[b] + seen
                sc_ref[c, blocks[b], :] = jnp.where(tied[b] & (rank_from_end <= excess[0:1, :]), jnp.nan, kc[b])
                seen = rank_from_end[0:1, :]
            return j + 1, seen, jnp.max(excess[0:1, :] - seen) > 0.0

        lax.while_loop(lambda carry: (carry[0] < n_chunks) & carry[2], drop_body,
                       (jnp.int32(0), jnp.zeros((1, QB), F32), jnp.bool_(True)))

    acc_ref[...] = jnp.zeros(acc_ref.shape, F32)

    def attend_chunks(chunks, m):
        s = []
        for c in chunks:
            start = pl.multiple_of(c * TK, TK)
            bias = jnp.where(sc_ref[c] >= thr_full, 0.0, NEG)
            bias4 = jnp.concatenate([bias] * KV_GROUP, axis=1)
            s.append([_dot_nt(kb_ref[pl.ds(start, TK), g * HEAD_DIM:(g + 1) * HEAD_DIM], q4_ref[g]) + bias4
                      for g in range(N_KV_HEADS)])
        m = list(m)
        for c, s_c in zip(chunks, s):
            for g in range(N_KV_HEADS):
                m_new = jnp.maximum(m[g], jnp.max(s_c[g], axis=0, keepdims=True))
                p = jnp.exp2(s_c[g] - m_new).astype(BF16)
                pv = _dot(vt_ref[c, g * VT_ROWS:(g + 1) * VT_ROWS, :], p)
                acc_ref[g] = jnp.exp2(m[g] - m_new) * acc_ref[g] + pv
                m[g] = m_new
        return tuple(m)

    m = (jnp.full((1, KV_GROUP * QB), NEG, F32),) * N_KV_HEADS
    m = lax.fori_loop(0, n_chunks // 2, lambda j, m: attend_chunks((2 * j, 2 * j + 1), m), m)
    lax.fori_loop(0, n_chunks % 2, lambda _, m: attend_chunks((n_chunks - 1,), m), m)

    heads = []
    for g in range(N_KV_HEADS):
        og = acc_ref[g, 0:HEAD_DIM, :] / acc_ref[g, HEAD_DIM:HEAD_DIM + 1, :]
        heads += [og[:, hh * QB:(hh + 1) * QB].T for hh in range(KV_GROUP)]
    o = jnp.concatenate(heads, axis=1).astype(BF16)
    y = _dot(o, w_out_ref[...])
    o_ref[...] = _post_norm(x_ref[...], y, _prompt_mod(mod_ref, 2), lng_ref[...], lnb_ref[...])


def _attend_prompt_call(x, mod_p, lng, lnb, q, qi_hm, wit, kb, vt, kib, w_out):
    bsz, t, _ = x.shape
    tok = lambda b, i: (b, i, 0)
    seq = lambda b, i: (b, 0, 0)
    return pl.pallas_call(
        _attend_prompt_kernel,
        grid=(bsz, t // QB),
        in_specs=[
            pl.BlockSpec((None, QB, D_MODEL), tok),
            _resident_spec(mod_p), _const_spec((1, D_MODEL)), _const_spec((1, D_MODEL)),
            pl.BlockSpec((None, QB, Q_COLS), tok),
            pl.BlockSpec((None, N_IDX_HEADS, QB, LANE), lambda b, i: (b, 0, i, 0)),
            pl.BlockSpec((None, N_IDX_HEADS, QB), lambda b, i: (b, 0, i)),
            pl.BlockSpec((None, t, KV_COLS), seq),
            pl.BlockSpec((None, t // TK, N_KV_HEADS * VT_ROWS, TK), lambda b, i: (b, 0, 0, 0)),
            pl.BlockSpec((None, t, LANE), seq),
            _resident_spec(w_out),
        ],
        out_specs=pl.BlockSpec((None, QB, D_MODEL), tok),
        out_shape=jax.ShapeDtypeStruct(x.shape, F32),
        scratch_shapes=[
            pltpu.VMEM((t // TK, TK, QB), F32),
            pltpu.VMEM((N_KV_HEADS, KV_GROUP * QB, HEAD_DIM), BF16),
            pltpu.VMEM((N_KV_HEADS, VT_ROWS, KV_GROUP * QB), F32),
        ],
        compiler_params=_params(("arbitrary", "arbitrary")),
        name="dsa_attend_prompt",
    )(x, *_arrays((mod_p, _row(lng), _row(lnb), q, qi_hm, wit, kb, vt, kib, w_out)))


def _page_copies(pt_ref, cache_ref, layer, buf_ref, sem_ref, slot, first_sample, n_samples, n_pages, along_lanes=False):
    rows, cols = cache_ref.shape[2:]
    copies = []
    for bb in range(n_samples):
        for p in range(n_pages):
            src = cache_ref.at[layer, pt_ref[first_sample + bb, p]]
            if along_lanes:
                dst = buf_ref.at[slot, bb, :, pl.ds(p * cols, cols)]
            else:
                dst = buf_ref.at[slot, bb, pl.ds(p * rows, rows), :]
            copies.append(pltpu.make_async_copy(src, dst, sem_ref.at[slot]))
    return copies


def _select_sample_kernel(pt_ref, qi_ref, wbc_ref, sn_ref, cache_ref, bias_ref, buf_ref, sc_ref, sem_ref,
                          *, layer, n_pages, page):
    step = pl.program_id(0)
    slot = step % 2
    past = n_pages * page
    width = past + LANE

    def copies(s, sl):
        return _page_copies(pt_ref, cache_ref, layer, buf_ref, sem_ref, sl, s * SB, SB, n_pages, along_lanes=True)

    @pl.when(step == 0)
    def _():
        for cp in copies(0, 0):
            cp.start()

    @pl.when(step + 1 < pl.num_programs(0))
    def _():
        for cp in copies(step + 1, 1 - slot):
            cp.start()

    for cp in copies(step, slot):
        cp.wait()

    for bb in range(SB):
        qs = qi_ref[bb][:, :IDX_DIM].astype(BF16)
        s = _dot(qs, buf_ref[slot, bb].astype(BF16))
        w = jnp.concatenate([wbc_ref[bb]] * (past // LANE), axis=1)
        sidx = jnp.sum(jnp.maximum(s, 0.0) * w, axis=0, keepdims=True)
        sc_ref[bb:bb + 1, 0:past] = sidx
    lane = lax.broadcasted_iota(I32, (SB, LANE), 1)
    sc_ref[:, past:width] = jnp.where(lane == 0, sn_ref[...], jnp.nan)

    lane_tiles = width // LANE
    sc = sc_ref[...]

    def count_ge(cand):
        acc = jnp.zeros((SB, LANE), F32)
        for t in range(lane_tiles):
            acc = acc + jnp.where(sc[:, t * LANE:(t + 1) * LANE] >= cand, 1.0, 0.0)
        return jnp.sum(acc, axis=1, keepdims=True)

    thr, excess = _kth_threshold(count_ge, (SB, LANE), TOPK_MAX)
    thr_full = jnp.concatenate([thr] * lane_tiles, axis=1)
    bias_ref[...] = jnp.where(sc >= thr_full, 0.0, NEG)

    @pl.when(jnp.max(excess) > 0.0)
    def _():
        later_or_same = (lax.broadcasted_iota(I32, (LANE, LANE), 0) >= lax.broadcasted_iota(I32, (LANE, LANE), 1)).astype(BF16)
        seen = jnp.zeros((SB, 1), F32)
        for t in reversed(range(lane_tiles)):
            cols = slice(t * LANE, (t + 1) * LANE)
            tied = sc[:, cols] == thr
            rank_from_end = _dot(jnp.where(tied, 1.0, 0.0).astype(BF16), later_or_same) + seen
            bias_ref[:, cols] = jnp.where(tied & (rank_from_end <= excess), NEG, bias_ref[:, cols])
            seen = rank_from_end[:, 0:1]


def _select_sample_call(page_table, qi3, wbc, sn, cache_kidx_t, layer):
    n, n_pages = page_table.shape
    page = cache_kidx_t.shape[3]
    width = n_pages * page + LANE
    kern = functools.partial(_select_sample_kernel, layer=layer, n_pages=n_pages, page=page)
    return pl.pallas_call(
        kern,
        grid_spec=pltpu.PrefetchScalarGridSpec(
            num_scalar_prefetch=1,
            grid=(n // SB,),
            in_specs=[
                pl.BlockSpec((SB, N_IDX_HEADS, LANE), lambda s, pt: (s, 0, 0)),
                pl.BlockSpec((SB, N_IDX_HEADS, LANE), lambda s, pt: (s, 0, 0)),
                pl.BlockSpec((SB, LANE), lambda s, pt: (s, 0)),
                pl.BlockSpec(memory_space=pl.ANY),
            ],
            out_specs=pl.BlockSpec((SB, width), lambda s, pt: (s, 0)),
            scratch_shapes=[
                pltpu.VMEM((2, SB, IDX_DIM, n_pages * page), F32),
                pltpu.VMEM((SB, width), F32),
                pltpu.SemaphoreType.DMA((2,)),
            ],
        ),
        out_shape=jax.ShapeDtypeStruct((n, width), F32),
        compiler_params=_params(("arbitrary",)),
        name="dsa_select_sample",
    )(page_table, qi3, wbc, sn, cache_kidx_t)


def _attend_sample_kernel(pt_ref, q_ref, kn_ref, vn_ref, bias_ref, ck_ref, cv_ref, o_ref,
                          kbuf_ref, vbuf_ref, ksem_ref, vsem_ref, *, layer, n_pages, page):
    b = pl.program_id(0)
    slot = b % 2
    past = n_pages * page

    def copies(sample, sl):
        return (_page_copies(pt_ref, ck_ref, layer, kbuf_ref, ksem_ref, sl, sample, 1, n_pages)
                + _page_copies(pt_ref, cv_ref, layer, vbuf_ref, vsem_ref, sl, sample, 1, n_pages))

    @pl.when(b == 0)
    def _():
        for cp in copies(0, 0):
            cp.start()

    @pl.when(b + 1 < pl.num_programs(0))
    def _():
        for cp in copies(b + 1, 1 - slot):
            cp.start()

    for cp in copies(b, slot):
        cp.wait()

    qb = q_ref[...].astype(BF16)
    qr = qb.astype(F32)
    bias = bias_ref[...]
    head = lax.broadcasted_iota(I32, (N_HEADS, HEAD_DIM), 0)
    rows = [pl.ds(g, past, stride=N_KV_HEADS) for g in range(N_KV_HEADS)]
    s = [_dot_nt(qb, kbuf_ref[slot, 0, rows[g], :].astype(BF16)) + bias[:, :past] for g in range(N_KV_HEADS)]
    p, p_new, denom = [], [], []
    for g in range(N_KV_HEADS):
        k_new = kn_ref[g:g + 1, :].astype(BF16).astype(F32)
        s_new = jnp.sum(qr * k_new, axis=1, keepdims=True) + bias[:, past:past + 1]
        m = jnp.maximum(jnp.max(s[g], axis=1, keepdims=True), s_new)
        p.append(jnp.exp(s[g] - m))
        p_new.append(jnp.exp(s_new - m))
        denom.append(jnp.sum(p[g], axis=1, keepdims=True) + p_new[g])
    og = [_dot(p[g].astype(BF16), vbuf_ref[slot, 0, rows[g], :].astype(BF16)) for g in range(N_KV_HEADS)]
    o = jnp.zeros((N_HEADS, HEAD_DIM), F32)
    for g in range(N_KV_HEADS):
        v_new = vn_ref[g:g + 1, :].astype(BF16).astype(F32)
        o_g = (og[g] + p_new[g].astype(BF16).astype(F32) * v_new) / denom[g]
        o = jnp.where(head // KV_GROUP == g, o_g, o)
    o_ref[...] = o


def _attend_sample_call(page_table, q3, kn3, vn3, bias3, cache_k, cache_v, layer):
    n, n_pages = page_table.shape
    page = cache_k.shape[2] // N_KV_HEADS
    past = n_pages * page
    kern = functools.partial(_attend_sample_kernel, layer=layer, n_pages=n_pages, page=page)
    per = lambda b, pt: (b, 0, 0)
    return pl.pallas_call(
        kern,
        grid_spec=pltpu.PrefetchScalarGridSpec(
            num_scalar_prefetch=1,
            grid=(n,),
            in_specs=[
                pl.BlockSpec((None, N_HEADS, HEAD_DIM), per),
                pl.BlockSpec((None, N_KV_HEADS, HEAD_DIM), per),
                pl.BlockSpec((None, N_KV_HEADS, HEAD_DIM), per),
                pl.BlockSpec((None, 1, past + LANE), per),
                pl.BlockSpec(memory_space=pl.ANY),
                pl.BlockSpec(memory_space=pl.ANY),
            ],
            out_specs=pl.BlockSpec((None, N_HEADS, HEAD_DIM), per),
            scratch_shapes=[
                pltpu.VMEM((2, 1, N_KV_HEADS * past, HEAD_DIM), F32),
                pltpu.VMEM((2, 1, N_KV_HEADS * past, HEAD_DIM), F32),
                pltpu.SemaphoreType.DMA((2,)),
                pltpu.SemaphoreType.DMA((2,)),
            ],
        ),
        out_shape=jax.ShapeDtypeStruct((n, N_HEADS, HEAD_DIM), F32),
        compiler_params=_params(("arbitrary",)),
        name="dsa_attend_sample",
    )(page_table, q3, kn3, vn3, bias3, cache_k, cache_v)


def _out_sample_kernel(x_ref, mod_ref, lng_ref, lnb_ref, o_ref, w_out_ref, y_ref):
    y = _dot(o_ref[...].astype(BF16), w_out_ref[...])
    y_ref[...] = _post_norm(x_ref[...], y, mod_ref[2], lng_ref[...], lnb_ref[...])


def _out_sample_call(x, mod_s, lng, lnb, o, w_out):
    args = (x, mod_s, _row(lng), _row(lnb), o, w_out)
    return pl.pallas_call(
        _out_sample_kernel,
        grid=(1,),
        in_specs=[_resident_spec(a) for a in args],
        out_specs=_full_spec(x.shape),
        out_shape=jax.ShapeDtypeStruct(x.shape, F32),
        compiler_params=_params(("arbitrary",)),
        name="dsa_out_sample",
    )(*_arrays(args))


def _pack_proj_weight(w_in):
    d = w_in.shape[0]
    o_qi = Q_COLS + 2 * KV_COLS
    o_ki = o_qi + QI_COLS
    o_wi = o_ki + IDX_DIM
    qi = w_in[:, o_qi:o_ki].reshape(d, N_IDX_HEADS, IDX_DIM)
    qi = jnp.pad(qi, ((0, 0), (0, 0), (0, LANE - IDX_DIM))).reshape(d, QI_PAD)
    ki = jnp.pad(w_in[:, o_ki:o_wi], ((0, 0), (0, LANE - IDX_DIM)))
    wi = jnp.pad(w_in[:, o_wi:], ((0, 0), (0, LANE - N_IDX_HEADS)))
    return jnp.concatenate([w_in[:, :o_qi], qi, ki, wi], axis=1).astype(BF16)


def _pack_proj_weight_t(w_in):
    o_v = Q_COLS + KV_COLS
    o_wi = Q_COLS + 2 * KV_COLS + QI_COLS + IDX_DIM
    wt = jnp.concatenate([w_in[:, o_v:o_v + KV_COLS], w_in[:, o_wi:]], axis=1).T
    return jnp.pad(wt, ((0, 2 * SUBLANE - N_IDX_HEADS), (0, 0))).astype(BF16)


def kernel(x_prompt, x_sample, cache_k, cache_v, cache_kidx, state_conv, page_table, c_prompt, c_sample,
           w_ada, b_ada, ln_g, ln_b, sgu_w_in, sgu_b_in, sgu_norm_g, sgu_norm_b, sgu_w_s, sgu_b_s, sgu_w_out,
           dsa_w_in, dsa_w_out, ffn_w_up, ffn_conv_w, ffn_conv_b, ffn_w_down):
    bsz, t_p, _ = x_prompt.shape
    n_s = x_sample.shape[0]
    n_phys, page = cache_k.shape[1], cache_k.shape[2]
    past = page_table.shape[1] * page

    c_prompt8 = jnp.pad(c_prompt, ((0, SUBLANE - bsz), (0, 0)))
    mods_s, mods_p = _ada_call(c_sample, c_prompt8, w_ada, b_ada)

    ck = cache_k.reshape(cache_k.shape[0], n_phys, page * N_KV_HEADS, HEAD_DIM)
    cv = cache_v.reshape(cache_v.shape[0], n_phys, page * N_KV_HEADS, HEAD_DIM)
    ckidx_t = jnp.swapaxes(cache_kidx, 2, 3)

    sgu_w_in_b, sgu_w_out_b = sgu_w_in.astype(BF16), sgu_w_out.astype(BF16)
    dsa_w_out_b = dsa_w_out.astype(BF16)
    ffn_w_up_b, ffn_w_down_b = ffn_w_up.astype(BF16), ffn_w_down.astype(BF16)
    n_dsa = DEPTH // 2

    xp = x_prompt
    xs = x_sample.reshape(n_s, D_MODEL)
    kv_stacks = tuple(jnp.zeros((n_dsa, bsz, N_KV_HEADS * t_p, HEAD_DIM), F32) for _ in range(2))
    kip_l, ks_l, vs_l, kis_l, sgu_l, convp_l, convs_l = [], [], [], [], [], [], []
    for i in range(DEPTH):
        j = i // 2
        mod_p, mod_s = (mods_p, i), (mods_s, i)
        if i % 2 == 0:
            w_in = (sgu_w_in_b, j)
            w_out = (sgu_w_out_b, j)
            wtril = jnp.tril(sgu_w_s[j]).astype(BF16)
            bs_full = jnp.repeat(sgu_b_s[j].T, SGU_GROUP_DIM, axis=1)
            ws0 = jnp.repeat(sgu_w_s[j][:, 0, 0], SGU_GROUP_DIM)
            bs0 = jnp.repeat(sgu_b_s[j][:, 0], SGU_GROUP_DIM)
            xp = _sgu_prompt_call(xp, mod_p, ln_g[i, 0], ln_b[i, 0], w_in, sgu_b_in[j], sgu_norm_g[j],
                                  sgu_norm_b[j], wtril, bs_full, w_out)
            xs, v_rows = _sgu_sample_call(xs, mod_s, ln_g[i, 0], ln_b[i, 0], w_in, sgu_b_in[j], sgu_norm_g[j],
                                          sgu_norm_b[j], ws0, bs0, w_out)
            sgu_l.append(v_rows.reshape(n_s, 1, D_SGU))
        else:
            w_proj = _pack_proj_weight(dsa_w_in[j])
            w_out = (dsa_w_out_b, j)
            q, k_stack, v_stack, kb, vt, qi_hm, ki, kib, wit = _proj_prompt_call(
                xp, mod_p, w_proj, _pack_proj_weight_t(dsa_w_in[j]), j, n_dsa, kv_stacks)
            kv_stacks = (k_stack, v_stack)
            xp = _attend_prompt_call(xp, mod_p, ln_g[i, 0], ln_b[i, 0], q, qi_hm, wit, kb, vt, kib, w_out)
            kip_l.append(ki)

            qs, ks_new, vs_new, qis, kis_new, wis, sn = _proj_sample_call(xs, mod_s, w_proj)
            wbc = jnp.broadcast_to(wis[:, :N_IDX_HEADS, None], (n_s, N_IDX_HEADS, LANE))
            bias = _select_sample_call(page_table, qis.reshape(n_s, N_IDX_HEADS, LANE), wbc, sn, ckidx_t, j)
            o = _attend_sample_call(page_table, qs.reshape(n_s, N_HEADS, HEAD_DIM),
                                    ks_new.reshape(n_s, N_KV_HEADS, HEAD_DIM),
                                    vs_new.reshape(n_s, N_KV_HEADS, HEAD_DIM),
                                    bias.reshape(n_s, 1, past + LANE), ck, cv, j)
            xs = _out_sample_call(xs, mod_s, ln_g[i, 0], ln_b[i, 0], o.reshape(n_s, Q_COLS), w_out)
            ks_l.append(ks_new.reshape(n_s, 1, N_KV_HEADS, HEAD_DIM))
            vs_l.append(vs_new.reshape(n_s, 1, N_KV_HEADS, HEAD_DIM))
            kis_l.append(kis_new.reshape(n_s, 1, IDX_DIM))

        w_up = (ffn_w_up_b, i)
        w_down = (ffn_w_down_b, i)
        xp, conv_p = _ffn_prompt_call(xp, mod_p, ln_g[i, 1], ln_b[i, 1], w_up, ffn_conv_w[i], ffn_conv_b[i], w_down)
        past_t = jnp.swapaxes(state_conv[i], 0, 1)
        xs, a_s = _ffn_sample_call(xs, mod_s, ln_g[i, 1], ln_b[i, 1], w_up, ffn_conv_w[i], ffn_conv_b[i], w_down,
                                   past_t)
        convp_l.append(conv_p)
        convs_l.append(jnp.stack([state_conv[i][:, 1], a_s], axis=1))

    return (xp, xs.reshape(n_s, 1, D_MODEL),
            kv_stacks[0].reshape(n_dsa, bsz, t_p, N_KV_HEADS, HEAD_DIM),
            kv_stacks[1].reshape(n_dsa, bsz, t_p, N_KV_HEADS, HEAD_DIM), jnp.stack(kip_l),
            jnp.stack(ks_l), jnp.stack(vs_l), jnp.stack(kis_l),
            jnp.stack(sgu_l), jnp.stack(convp_l), jnp.stack(convs_l))
```

```python
import functools

import jax
import jax.numpy as jnp
from jax import lax
from jax.experimental import pallas as pl
from jax.experimental.pallas import tpu as pltpu

F32 = jnp.float32
BF16 = jnp.bfloat16
I32 = jnp.int32

D_MODEL = 1024
DEPTH = 4
N_MOD = 6
CHUNK = 128
D_SGU = D_MODEL
SGU_GROUPS = 8
SGU_GROUP_DIM = D_SGU // SGU_GROUPS
N_HEADS = 8
HEAD_DIM = D_MODEL // N_HEADS
N_KV_HEADS = 2
KV_GROUP = N_HEADS // N_KV_HEADS
N_IDX_HEADS = 8
IDX_DIM = 64
TOPK_MAX = 256
Q_COLS = N_HEADS * HEAD_DIM
KV_COLS = N_KV_HEADS * HEAD_DIM
QI_COLS = N_IDX_HEADS * IDX_DIM
D_FF = 2816
CONV_W = 3
ALPHA = (2 * DEPTH) ** 0.25
LN_EPS = 1e-5

LANE = 128
SUBLANE = 8
VMEM_LIMIT = 56 * 1024 * 1024

TM = 512
FF_CHUNK = 256
N_FF_CHUNKS = D_FF // FF_CHUNK
QB = 128
TK = 512
TIE_BLOCK = 256
KEY_MIN = -2 ** 31
NEG = -1e30
QI_PAD = N_IDX_HEADS * LANE
PROJ_COLS = Q_COLS + 2 * KV_COLS + QI_PAD + 2 * LANE
SB = 16
LOG2E = 1.4426950408889634
VT_ROWS = HEAD_DIM + 16


def _dot(a, b):
    return jnp.dot(a, b, preferred_element_type=F32)


def _dot_nt(a, b):
    return lax.dot_general(a, b, (((1,), (1,)), ((), ())), preferred_element_type=F32)


def _ln(x):
    mu = jnp.mean(x, axis=-1, keepdims=True)
    xc = x - mu
    var = jnp.mean(xc * xc, axis=-1, keepdims=True)
    return xc * lax.rsqrt(var + LN_EPS)


def _modulate(x, shift, scale):
    return x * (1.0 + scale) + shift


def _post_norm(x, y, gate, g, b):
    return _ln(ALPHA * x + (1.0 + gate) * y) * g + b


def _prompt_mod(mod_ref, m):
    return mod_ref[m, pl.ds(pl.program_id(0), 1), :]


def _key_value(key):
    bits = jnp.where(key < 0, jnp.int32(KEY_MIN) - key, key)
    return lax.bitcast_convert_type(bits, F32)


def _kth_threshold(count_ge, shape, kth):
    def bit_body(it, carry):
        key, cnt = carry
        cand = key + jnp.left_shift(jnp.int32(1), 31 - it)
        c = jnp.broadcast_to(count_ge(_key_value(cand)), shape)
        ok = c >= kth
        return jnp.where(ok, cand, key), jnp.where(ok, c, cnt)

    key, cnt = lax.fori_loop(0, 32, bit_body, (jnp.full(shape, KEY_MIN, I32), jnp.full(shape, kth, F32)))
    return jnp.where(key == KEY_MIN, -jnp.inf, _key_value(key)), cnt - kth


def _params(sem=None):
    return pltpu.CompilerParams(dimension_semantics=sem, vmem_limit_bytes=VMEM_LIMIT)


def _const_spec(shape):
    return pl.BlockSpec(shape, lambda *_: (0,) * len(shape), pipeline_mode=pl.Buffered(1))


def _full_spec(shape):
    return pl.BlockSpec(shape, lambda *_: (0,) * len(shape))


def _resident_spec(a):
    if isinstance(a, tuple):
        arr, layer = a
        tail = (0,) * (arr.ndim - 1)
        return pl.BlockSpec((None,) + arr.shape[1:], lambda *_: (layer,) + tail, pipeline_mode=pl.Buffered(1))
    return _const_spec(a.shape)


def _arrays(args):
    return [a[0] if isinstance(a, tuple) else a for a in args]


def _ada_kernel(cs_ref, cp_ref, w_ref, b_ref, os_ref, op_ref):
    w = w_ref[...].astype(BF16)
    bias = b_ref[...]
    os_ref[...] = _dot(jax.nn.silu(cs_ref[...]).astype(BF16), w) + bias
    op_ref[...] = _dot(jax.nn.silu(cp_ref[...]).astype(BF16), w) + bias


def _ada_call(c_sample, c_prompt8, w_ada, b_ada):
    tn = 512
    nn = D_MODEL // tn
    n_s = c_sample.shape[0]
    return pl.pallas_call(
        _ada_kernel,
        grid=(DEPTH, N_MOD, nn),
        in_specs=[
            pl.BlockSpec((n_s, D_MODEL), lambda l, m, n: (0, 0)),
            pl.BlockSpec((SUBLANE, D_MODEL), lambda l, m, n: (0, 0)),
            pl.BlockSpec((None, D_MODEL, tn), lambda l, m, n: (l, 0, m * nn + n)),
            pl.BlockSpec((None, 1, tn), lambda l, m, n: (l, 0, m * nn + n)),
        ],
        out_specs=[
            pl.BlockSpec((None, None, n_s, tn), lambda l, m, n: (l, m, 0, n)),
            pl.BlockSpec((None, None, SUBLANE, tn), lambda l, m, n: (l, m, 0, n)),
        ],
        out_shape=[
            jax.ShapeDtypeStruct((DEPTH, N_MOD, n_s, D_MODEL), F32),
            jax.ShapeDtypeStruct((DEPTH, N_MOD, SUBLANE, D_MODEL), F32),
        ],
        compiler_params=_params(("arbitrary",) * 3),
        name="ada_params",
    )(c_sample, c_prompt8, w_ada, b_ada.reshape(DEPTH, 1, N_MOD * D_MODEL))


def _sgu_front(x, shift, scale, w_in_ref, b_in_ref, ng_ref, nb_ref):
    h = _modulate(x, shift, scale).astype(BF16)
    u = jax.nn.gelu(_dot(h, w_in_ref[:, :D_SGU]) + b_in_ref[:, :D_SGU])
    v = jax.nn.gelu(_dot(h, w_in_ref[:, D_SGU:]) + b_in_ref[:, D_SGU:])
    v = _ln(v) * ng_ref[...] + nb_ref[...]
    return u, v


def _sgu_prompt_kernel(x_ref, mod_ref, lng_ref, lnb_ref, w_in_ref, b_in_ref, ng_ref, nb_ref,
                       wtril_ref, bs_ref, w_out_ref, o_ref, gated_ref):
    x = x_ref[...]
    u, v = _sgu_front(x, _prompt_mod(mod_ref, 0), _prompt_mod(mod_ref, 1),
                      w_in_ref, b_in_ref, ng_ref, nb_ref)
    vb = v.astype(BF16)
    n_chunks = x.shape[0] // CHUNK
    for g in range(SGU_GROUPS):
        cols = slice(g * SGU_GROUP_DIM, (g + 1) * SGU_GROUP_DIM)
        rhs = jnp.concatenate([vb[n * CHUNK:(n + 1) * CHUNK, cols] for n in range(n_chunks)], axis=1)
        mixed = _dot(wtril_ref[g], rhs)
        for n in range(n_chunks):
            rows = slice(n * CHUNK, (n + 1) * CHUNK)
            mix_n = mixed[:, n * SGU_GROUP_DIM:(n + 1) * SGU_GROUP_DIM] + bs_ref[:, cols]
            gated_ref[rows, cols] = (u[rows, cols] * mix_n).astype(BF16)
    y = _dot(gated_ref[...], w_out_ref[...])
    o_ref[...] = _post_norm(x, y, _prompt_mod(mod_ref, 2), lng_ref[...], lnb_ref[...])


def _sgu_sample_kernel(x_ref, mod_ref, lng_ref, lnb_ref, w_in_ref, b_in_ref, ng_ref, nb_ref,
                       ws0_ref, bs0_ref, w_out_ref, o_ref, v_ref):
    x = x_ref[...]
    u, v = _sgu_front(x, mod_ref[0], mod_ref[1], w_in_ref, b_in_ref, ng_ref, nb_ref)
    v_ref[...] = v
    mixed = v * ws0_ref[...] + bs0_ref[...]
    y = _dot((u * mixed).astype(BF16), w_out_ref[...])
    o_ref[...] = _post_norm(x, y, mod_ref[2], lng_ref[...], lnb_ref[...])


def _row(v):
    return v.reshape(1, -1)


def _sgu_prompt_call(x, mod_p, lng, lnb, w_in, b_in, ng, nb, wtril, bs_full, w_out):
    bsz, t, _ = x.shape
    resident = (mod_p, _row(lng), _row(lnb), w_in, _row(b_in), _row(ng), _row(nb), wtril, bs_full, w_out)
    return pl.pallas_call(
        _sgu_prompt_kernel,
        grid=(bsz, t // TM),
        in_specs=[pl.BlockSpec((None, TM, D_MODEL), lambda b, i: (b, i, 0))] + [_resident_spec(a) for a in resident],
        out_specs=pl.BlockSpec((None, TM, D_MODEL), lambda b, i: (b, i, 0)),
        out_shape=jax.ShapeDtypeStruct(x.shape, F32),
        scratch_shapes=[pltpu.VMEM((TM, D_SGU), BF16)],
        compiler_params=_params(("arbitrary", "arbitrary")),
        name="sgu_prompt",
    )(x, *_arrays(resident))


def _sgu_sample_call(x, mod_s, lng, lnb, w_in, b_in, ng, nb, ws0, bs0, w_out):
    n = x.shape[0]
    args = (x, mod_s, _row(lng), _row(lnb), w_in, _row(b_in), _row(ng), _row(nb), _row(ws0), _row(bs0), w_out)
    return pl.pallas_call(
        _sgu_sample_kernel,
        grid=(1,),
        in_specs=[_resident_spec(a) for a in args],
        out_specs=[_full_spec((n, D_MODEL)), _full_spec((n, D_SGU))],
        out_shape=[jax.ShapeDtypeStruct((n, D_MODEL), F32), jax.ShapeDtypeStruct((n, D_SGU), F32)],
        compiler_params=_params(("arbitrary",)),
        name="sgu_sample",
    )(*_arrays(args))


def _ffn_prompt_kernel(x_ref, mod_ref, lng_ref, lnb_ref, w_up_ref, cw_ref, cb_ref, w_down_ref,
                       o_ref, st_ref, carry_ref, g_ref, h_ref):
    i = pl.program_id(1)
    tm = x_ref.shape[0]

    @pl.when(i == 0)
    def _():
        carry_ref[...] = jnp.zeros(carry_ref.shape, F32)

    h_ref[...] = _modulate(x_ref[...], _prompt_mod(mod_ref, 3), _prompt_mod(mod_ref, 4)).astype(BF16)

    def up(c):
        a = _dot(h_ref[...], w_up_ref[:, c * FF_CHUNK:(c + 1) * FF_CHUNK])
        u = _dot(h_ref[...], w_up_ref[:, D_FF + c * FF_CHUNK:D_FF + (c + 1) * FF_CHUNK])
        return a, u

    head_row = lax.broadcasted_iota(I32, (SUBLANE, FF_CHUNK), 0)
    nxt = up(0)
    for c in range(N_FF_CHUNKS):
        cols = slice(c * FF_CHUNK, (c + 1) * FF_CHUNK)
        a, u = nxt
        if c + 1 < N_FF_CHUNKS:
            nxt = up(c + 1)
        prev = carry_ref[c]
        shifted = []
        for j in range(1, CONV_W):
            rolled = pltpu.roll(a, j, axis=0)
            head = jnp.where(head_row < j, pltpu.roll(prev, j, axis=0), rolled[0:SUBLANE, :])
            shifted.append(jnp.concatenate([head, rolled[SUBLANE:, :]], axis=0))
        a_m1, a_m2 = shifted
        conv = a_m2 * cw_ref[0:1, cols] + a_m1 * cw_ref[1:2, cols] + a * cw_ref[2:3, cols] + cb_ref[:, cols]
        carry_ref[c] = a[tm - SUBLANE:tm, :]
        g_ref[:, cols] = (jax.nn.gelu(conv) * u).astype(BF16)

    @pl.when(i == pl.num_programs(1) - 1)
    def _():
        for c in range(N_FF_CHUNKS):
            st_ref[:, c * FF_CHUNK:(c + 1) * FF_CHUNK] = carry_ref[c, SUBLANE - (CONV_W - 1):SUBLANE, :]

    gate, lng, lnb = _prompt_mod(mod_ref, 5), lng_ref[...], lnb_ref[...]
    half = tm // 2
    for r in range(2):
        rows = slice(r * half, (r + 1) * half)
        y = _dot(g_ref[rows, :], w_down_ref[...])
        o_ref[rows, :] = _post_norm(x_ref[rows, :], y, gate, lng, lnb)


def _ffn_sample_kernel(x_ref, mod_ref, lng_ref, lnb_ref, w_up_ref, cw_ref, cb_ref, w_down_ref, past_ref,
                       o_ref, a_ref, g_ref):
    x = x_ref[...]
    h = _modulate(x, mod_ref[3], mod_ref[4]).astype(BF16)
    for c in range(N_FF_CHUNKS):
        cols = slice(c * FF_CHUNK, (c + 1) * FF_CHUNK)
        ucols = slice(D_FF + c * FF_CHUNK, D_FF + (c + 1) * FF_CHUNK)
        a = _dot(h, w_up_ref[:, cols])
        a_ref[:, cols] = a
        conv = (past_ref[0, :, cols] * cw_ref[0:1, cols] + past_ref[1, :, cols] * cw_ref[1:2, cols]
                + a * cw_ref[2:3, cols] + cb_ref[:, cols])
        u = _dot(h, w_up_ref[:, ucols])
        g_ref[:, cols] = (jax.nn.gelu(conv) * u).astype(BF16)
    y = _dot(g_ref[...], w_down_ref[...])
    o_ref[...] = _post_norm(x, y, mod_ref[5], lng_ref[...], lnb_ref[...])


def _ffn_prompt_call(x, mod_p, lng, lnb, w_up, cw, cb, w_down):
    bsz, t, _ = x.shape
    resident = (mod_p, _row(lng), _row(lnb), w_up, cw, _row(cb), w_down)
    return pl.pallas_call(
        _ffn_prompt_kernel,
        grid=(bsz, t // TM),
        in_specs=[pl.BlockSpec((None, TM, D_MODEL), lambda b, i: (b, i, 0))] + [_resident_spec(a) for a in resident],
        out_specs=[
            pl.BlockSpec((None, TM, D_MODEL), lambda b, i: (b, i, 0)),
            pl.BlockSpec((None, CONV_W - 1, D_FF), lambda b, i: (b, 0, 0)),
        ],
        out_shape=[jax.ShapeDtypeStruct(x.shape, F32), jax.ShapeDtypeStruct((bsz, CONV_W - 1, D_FF), F32)],
        scratch_shapes=[pltpu.VMEM((N_FF_CHUNKS, SUBLANE, FF_CHUNK), F32), pltpu.VMEM((TM, D_FF), BF16),
                        pltpu.VMEM((TM, D_MODEL), BF16)],
        compiler_params=_params(("arbitrary", "arbitrary")),
        name="ffn_prompt",
    )(x, *_arrays(resident))


def _ffn_sample_call(x, mod_s, lng, lnb, w_up, cw, cb, w_down, past_t):
    n = x.shape[0]
    args = (x, mod_s, _row(lng), _row(lnb), w_up, cw, _row(cb), w_down, past_t)
    return pl.pallas_call(
        _ffn_sample_kernel,
        grid=(1,),
        in_specs=[_resident_spec(a) for a in args],
        out_specs=[_full_spec((n, D_MODEL)), _full_spec((n, D_FF))],
        out_shape=[jax.ShapeDtypeStruct((n, D_MODEL), F32), jax.ShapeDtypeStruct((n, D_FF), F32)],
        scratch_shapes=[pltpu.VMEM((n, D_FF), BF16)],
        compiler_params=_params(("arbitrary",)),
        name="ffn_sample",
    )(*_arrays(args))


_C_K = Q_COLS
_C_V = _C_K + KV_COLS
_C_QI = _C_V + KV_COLS
_C_KI = _C_QI + QI_PAD
_C_WI = _C_KI + LANE


def _proj_common(h, w_ref):
    q = _dot(h, w_ref[:, 0:_C_K]) * HEAD_DIM ** -0.5
    k = _dot(h, w_ref[:, _C_K:_C_V])
    v = _dot(h, w_ref[:, _C_V:_C_QI])
    qi = _dot(h, w_ref[:, _C_QI:_C_KI])
    ki = _dot(h, w_ref[:, _C_KI:_C_WI])
    wi = _dot(h, w_ref[:, _C_WI:PROJ_COLS]) * N_IDX_HEADS ** -0.5 * IDX_DIM ** -0.5
    return q, k, v, qi, ki, wi


def _store_kv_rows(ref, x):
    n = x.shape[0]
    for g in range(N_KV_HEADS):
        ref[pl.ds(g, n, stride=N_KV_HEADS), :] = x[:, g * HEAD_DIM:(g + 1) * HEAD_DIM]


def _proj_prompt_kernel(x_ref, mod_ref, w_ref, wt_ref, *refs, n_carried):
    q_ref, k_ref, v_ref, kb_ref, vt_ref, qi_ref, ki_ref, kib_ref, wit_ref = refs[n_carried:]
    h = _modulate(x_ref[...], _prompt_mod(mod_ref, 0), _prompt_mod(mod_ref, 1)).astype(BF16)
    q, k, v, qi, ki, _ = _proj_common(h, w_ref)
    q_ref[...] = (q * LOG2E).astype(BF16)
    _store_kv_rows(k_ref, k)
    _store_kv_rows(v_ref, v)
    kb_ref[...] = k.astype(BF16)
    for hh in range(N_IDX_HEADS):
        qi_ref[hh] = qi[:, hh * LANE:(hh + 1) * LANE].astype(BF16)
    ki_ref[...] = ki[:, :IDX_DIM]
    kib_ref[...] = ki.astype(BF16)
    vt = _dot_nt(wt_ref[0:KV_COLS, :], h).astype(BF16)
    ones_rows = (lax.broadcasted_iota(I32, (VT_ROWS - HEAD_DIM, vt.shape[1]), 0) == 0).astype(BF16)
    for g in range(N_KV_HEADS):
        vt_ref[0, g * VT_ROWS:g * VT_ROWS + HEAD_DIM, :] = vt[g * HEAD_DIM:(g + 1) * HEAD_DIM, :]
        vt_ref[0, g * VT_ROWS + HEAD_DIM:(g + 1) * VT_ROWS, :] = ones_rows
    wit = _dot_nt(wt_ref[KV_COLS:, :], h) * N_IDX_HEADS ** -0.5 * IDX_DIM ** -0.5
    wit_ref[...] = wit[0:N_IDX_HEADS, :]


def _proj_sample_kernel(x_ref, mod_ref, w_ref, q_ref, k_ref, v_ref, qi_ref, ki_ref, wi_ref, sn_ref):
    h = _modulate(x_ref[...], mod_ref[0], mod_ref[1]).astype(BF16)
    q, k, v, qi, ki, wi = _proj_common(h, w_ref)
    q_ref[...] = q
    _store_kv_rows(k_ref, k)
    _store_kv_rows(v_ref, v)
    qi_ref[...] = qi
    ki_ref[...] = ki[:, :IDX_DIM]
    wi_ref[...] = wi
    kr = ki.astype(BF16).astype(F32)
    sn = jnp.zeros((x_ref.shape[0], 1), F32)
    for hh in range(N_IDX_HEADS):
        qr = qi[:, hh * LANE:(hh + 1) * LANE].astype(BF16).astype(F32)
        sh = jnp.sum(qr * kr, axis=1, keepdims=True)
        sn = sn + jnp.maximum(sh, 0.0) * wi[:, hh:hh + 1]
    sn_ref[...] = jnp.broadcast_to(sn, sn_ref.shape)


def _proj_prompt_call(x, mod_p, w_proj, w_proj_t, slot, n_slots, kv_stacks):
    bsz, t, _ = x.shape
    assert TM == TK
    tok = lambda b, i: (b, i, 0)
    resident = (mod_p, w_proj, w_proj_t)
    carried = tuple(kv_stacks)
    kv_spec = pl.BlockSpec((None, None, N_KV_HEADS * TM, HEAD_DIM), lambda b, i: (slot, b, i, 0))
    kv_shape = jax.ShapeDtypeStruct((n_slots, bsz, N_KV_HEADS * t, HEAD_DIM), F32)
    return pl.pallas_call(
        functools.partial(_proj_prompt_kernel, n_carried=len(carried)),
        grid=(bsz, t // TM),
        in_specs=([pl.BlockSpec((None, TM, D_MODEL), tok)] + [_resident_spec(a) for a in resident]
                  + [pl.BlockSpec(memory_space=pl.ANY)] * len(carried)),
        input_output_aliases={1 + len(resident) + n: 1 + n for n in range(len(carried))},
        out_specs=[
            pl.BlockSpec((None, TM, Q_COLS), tok),
            kv_spec, kv_spec,
            pl.BlockSpec((None, TM, KV_COLS), tok),
            pl.BlockSpec((None, 1, N_KV_HEADS * VT_ROWS, TK), lambda b, i: (b, i, 0, 0)),
            pl.BlockSpec((None, N_IDX_HEADS, TM, LANE), lambda b, i: (b, 0, i, 0)),
            pl.BlockSpec((None, TM, IDX_DIM), tok),
            pl.BlockSpec((None, TM, LANE), tok),
            pl.BlockSpec((None, N_IDX_HEADS, TM), lambda b, i: (b, 0, i)),
        ],
        out_shape=[
            jax.ShapeDtypeStruct((bsz, t, Q_COLS), BF16),
            kv_shape, kv_shape,
            jax.ShapeDtypeStruct((bsz, t, KV_COLS), BF16),
            jax.ShapeDtypeStruct((bsz, t // TK, N_KV_HEADS * VT_ROWS, TK), BF16),
            jax.ShapeDtypeStruct((bsz, N_IDX_HEADS, t, LANE), BF16),
            jax.ShapeDtypeStruct((bsz, t, IDX_DIM), F32),
            jax.ShapeDtypeStruct((bsz, t, LANE), BF16),
            jax.ShapeDtypeStruct((bsz, N_IDX_HEADS, t), F32),
        ],
        compiler_params=_params(("arbitrary", "arbitrary")),
        name="dsa_proj_prompt",
    )(x, *_arrays(resident), *carried)


def _proj_sample_call(x, mod_s, w_proj):
    n = x.shape[0]
    args = (x, mod_s, w_proj)
    kv_shape = (N_KV_HEADS * n, HEAD_DIM)
    shapes = ((n, Q_COLS), kv_shape, kv_shape, (n, QI_PAD), (n, IDX_DIM), (n, LANE), (n, LANE))
    return pl.pallas_call(
        _proj_sample_kernel,
        grid=(1,),
        in_specs=[_resident_spec(a) for a in args],
        out_specs=[_full_spec(s) for s in shapes],
        out_shape=[jax.ShapeDtypeStruct(s, F32) for s in shapes],
        compiler_params=_params(("arbitrary",)),
        name="dsa_proj_sample",
    )(*_arrays(args))


def _attend_prompt_kernel(x_ref, mod_ref, lng_ref, lnb_ref, q_ref, qi_ref, wit_ref, kb_ref, vt_ref, kib_ref,
                          w_out_ref, o_ref, sc_ref, q4_ref, acc_ref):
    i = pl.program_id(1)
    n_chunks = (i * QB) // TK + 1
    sub_tiles = TK // SUBLANE

    for g in range(N_KV_HEADS):
        for hh in range(KV_GROUP):
            head = g * KV_GROUP + hh
            q4_ref[g, hh * QB:(hh + 1) * QB, :] = q_ref[:, head * HEAD_DIM:(head + 1) * HEAD_DIM]

    k_pos = lax.broadcasted_iota(I32, (TK, QB), 0)
    q_pos = i * QB + lax.broadcasted_iota(I32, (TK, QB), 1)

    def score_body(c, carry):
        start = pl.multiple_of(c * TK, TK)
        kslab = kib_ref[pl.ds(start, TK), :]
        sidx = jnp.zeros((TK, QB), F32)
        for pair in range(N_IDX_HEADS // 2):
            s = _dot_nt(kslab, qi_ref[2 * pair:2 * pair + 2].reshape(2 * QB, LANE))
            for j in range(2):
                hh = 2 * pair + j
                sidx = sidx + jnp.maximum(s[:, j * QB:(j + 1) * QB], 0.0) * wit_ref[hh:hh + 1, :]
        sc_ref[c] = jnp.where(k_pos + start <= q_pos, sidx, jnp.nan)
        return carry

    lax.fori_loop(0, n_chunks, score_body, 0)

    def count_ge(cand):
        def body(c, accs):
            kc = sc_ref[c]
            accs = list(accs)
            for t in range(sub_tiles):
                hit = jnp.where(kc[t * SUBLANE:(t + 1) * SUBLANE, :] >= cand, 1.0, 0.0)
                accs[t % len(accs)] = accs[t % len(accs)] + hit
            return tuple(accs)

        zero = jnp.zeros((SUBLANE, QB), F32)
        a0, a1, a2, a3 = lax.fori_loop(0, n_chunks, body, (zero,) * 4)
        return jnp.sum((a0 + a1) + (a2 + a3), axis=0, keepdims=True)

    thr, excess = _kth_threshold(count_ge, (SUBLANE, QB), TOPK_MAX)
    thr_full = jnp.concatenate([thr] * sub_tiles, axis=0)

    @pl.when(jnp.max(excess) > 0.0)
    def _():
        blk = TIE_BLOCK
        later_or_same = (lax.broadcasted_iota(I32, (blk, blk), 1) >= lax.broadcasted_iota(I32, (blk, blk), 0)).astype(BF16)

        def drop_body(carry):
            j, seen, _ = carry
            c = n_chunks - 1 - j
            blocks = [slice(b * blk, (b + 1) * blk) for b in range(TK // blk)]
            kc = [sc_ref[c, rows, :] for rows in blocks]
            tied = [k == thr_full[0:blk, :] for k in kc]
            local = [_dot(later_or_same, jnp.where(t, 1.0, 0.0).astype(BF16)) for t in tied]
            for b in reversed(range(len(blocks))):
                rank_from_end = local[b] + seen
                sc_ref[c, blocks[b], :] = jnp.where(tied[b] & (rank_from_end <= excess[0:1, :]), jnp.nan, kc[b])
                seen = rank_from_end[0:1, :]
            return j + 1, seen, jnp.max(excess[0:1, :] - seen) > 0.0

        lax.while_loop(lambda carry: (carry[0] < n_chunks) & carry[2], drop_body,
                       (jnp.int32(0), jnp.zeros((1, QB), F32), jnp.bool_(True)))

    acc_ref[...] = jnp.zeros(acc_ref.shape, F32)

    def attend_chunks(chunks, m):
        s = []
        for c in chunks:
            start = pl.multiple_of(c * TK, TK)
            bias = jnp.where(sc_ref[c] >= thr_full, 0.0, NEG)
            bias4 = jnp.concatenate([bias] * KV_GROUP, axis=1)
            s.append([_dot_nt(kb_ref[pl.ds(start, TK), g * HEAD_DIM:(g + 1) * HEAD_DIM], q4_ref[g]) + bias4
                      for g in range(N_KV_HEADS)])
        m = list(m)
        for c, s_c in zip(chunks, s):
            for g in range(N_KV_HEADS):
                m_new = jnp.maximum(m[g], jnp.max(s_c[g], axis=0, keepdims=True))
                p = jnp.exp2(s_c[g] - m_new).astype(BF16)
                pv = _dot(vt_ref[c, g * VT_ROWS:(g + 1) * VT_ROWS, :], p)
                acc_ref[g] = jnp.exp2(m[g] - m_new) * acc_ref[g] + pv
                m[g] = m_new
        return tuple(m)

    m = (jnp.full((1, KV_GROUP * QB), NEG, F32),) * N_KV_HEADS
    m = lax.fori_loop(0, n_chunks // 2, lambda j, m: attend_chunks((2 * j, 2 * j + 1), m), m)
    lax.fori_loop(0, n_chunks % 2, lambda _, m: attend_chunks((n_chunks - 1,), m), m)

    heads = []
    for g in range(N_KV_HEADS):
        og = acc_ref[g, 0:HEAD_DIM, :] / acc_ref[g, HEAD_DIM:HEAD_DIM + 1, :]
        heads += [og[:, hh * QB:(hh + 1) * QB].T for hh in range(KV_GROUP)]
    o = jnp.concatenate(heads, axis=1).astype(BF16)
    y = _dot(o, w_out_ref[...])
    o_ref[...] = _post_norm(x_ref[...], y, _prompt_mod(mod_ref, 2), lng_ref[...], lnb_ref[...])


def _attend_prompt_call(x, mod_p, lng, lnb, q, qi_hm, wit, kb, vt, kib, w_out):
    bsz, t, _ = x.shape
    tok = lambda b, i: (b, i, 0)
    seq = lambda b, i: (b, 0, 0)
    return pl.pallas_call(
        _attend_prompt_kernel,
        grid=(bsz, t // QB),
        in_specs=[
            pl.BlockSpec((None, QB, D_MODEL), tok),
            _resident_spec(mod_p), _const_spec((1, D_MODEL)), _const_spec((1, D_MODEL)),
            pl.BlockSpec((None, QB, Q_COLS), tok),
            pl.BlockSpec((None, N_IDX_HEADS, QB, LANE), lambda b, i: (b, 0, i, 0)),
            pl.BlockSpec((None, N_IDX_HEADS, QB), lambda b, i: (b, 0, i)),
            pl.BlockSpec((None, t, KV_COLS), seq),
            pl.BlockSpec((None, t // TK, N_KV_HEADS * VT_ROWS, TK), lambda b, i: (b, 0, 0, 0)),
            pl.BlockSpec((None, t, LANE), seq),
            _resident_spec(w_out),
        ],
        out_specs=pl.BlockSpec((None, QB, D_MODEL), tok),
        out_shape=jax.ShapeDtypeStruct(x.shape, F32),
        scratch_shapes=[
            pltpu.VMEM((t // TK, TK, QB), F32),
            pltpu.VMEM((N_KV_HEADS, KV_GROUP * QB, HEAD_DIM), BF16),
            pltpu.VMEM((N_KV_HEADS, VT_ROWS, KV_GROUP * QB), F32),
        ],
        compiler_params=_params(("arbitrary", "arbitrary")),
        name="dsa_attend_prompt",
    )(x, *_arrays((mod_p, _row(lng), _row(lnb), q, qi_hm, wit, kb, vt, kib, w_out)))


def _page_copies(pt_ref, cache_ref, layer, buf_ref, sem_ref, slot, first_sample, n_samples, n_pages, along_lanes=False):
    rows, cols = cache_ref.shape[2:]
    copies = []
    for bb in range(n_samples):
        for p in range(n_pages):
            src = cache_ref.at[layer, pt_ref[first_sample + bb, p]]
            if along_lanes:
                dst = buf_ref.at[slot, bb, :, pl.ds(p * cols, cols)]
            else:
                dst = buf_ref.at[slot, bb, pl.ds(p * rows, rows), :]
            copies.append(pltpu.make_async_copy(src, dst, sem_ref.at[slot]))
    return copies


def _select_sample_kernel(pt_ref, qi_ref, wbc_ref, sn_ref, cache_ref, bias_ref, buf_ref, sc_ref, sem_ref,
                          *, layer, n_pages, page):
    step = pl.program_id(0)
    slot = step % 2
    past = n_pages * page
    width = past + LANE

    def copies(s, sl):
        return _page_copies(pt_ref, cache_ref, layer, buf_ref, sem_ref, sl, s * SB, SB, n_pages, along_lanes=True)

    @pl.when(step == 0)
    def _():
        for cp in copies(0, 0):
            cp.start()

    @pl.when(step + 1 < pl.num_programs(0))
    def _():
        for cp in copies(step + 1, 1 - slot):
            cp.start()

    for cp in copies(step, slot):
        cp.wait()

    for bb in range(SB):
        qs = qi_ref[bb][:, :IDX_DIM].astype(BF16)
        s = _dot(qs, buf_ref[slot, bb].astype(BF16))
        w = jnp.concatenate([wbc_ref[bb]] * (past // LANE), axis=1)
        sidx = jnp.sum(jnp.maximum(s, 0.0) * w, axis=0, keepdims=True)
        sc_ref[bb:bb + 1, 0:past] = sidx
    lane = lax.broadcasted_iota(I32, (SB, LANE), 1)
    sc_ref[:, past:width] = jnp.where(lane == 0, sn_ref[...], jnp.nan)

    lane_tiles = width // LANE
    sc = sc_ref[...]

    def count_ge(cand):
        acc = jnp.zeros((SB, LANE), F32)
        for t in range(lane_tiles):
            acc = acc + jnp.where(sc[:, t * LANE:(t + 1) * LANE] >= cand, 1.0, 0.0)
        return jnp.sum(acc, axis=1, keepdims=True)

    thr, excess = _kth_threshold(count_ge, (SB, LANE), TOPK_MAX)
    thr_full = jnp.concatenate([thr] * lane_tiles, axis=1)
    bias_ref[...] = jnp.where(sc >= thr_full, 0.0, NEG)

    @pl.when(jnp.max(excess) > 0.0)
    def _():
        later_or_same = (lax.broadcasted_iota(I32, (LANE, LANE), 0) >= lax.broadcasted_iota(I32, (LANE, LANE), 1)).astype(BF16)
        seen = jnp.zeros((SB, 1), F32)
        for t in reversed(range(lane_tiles)):
            cols = slice(t * LANE, (t + 1) * LANE)
            tied = sc[:, cols] == thr
            rank_from_end = _dot(jnp.where(tied, 1.0, 0.0).astype(BF16), later_or_same) + seen
            bias_ref[:, cols] = jnp.where(tied & (rank_from_end <= excess), NEG, bias_ref[:, cols])
            seen = rank_from_end[:, 0:1]


def _select_sample_call(page_table, qi3, wbc, sn, cache_kidx_t, layer):
    n, n_pages = page_table.shape
    page = cache_kidx_t.shape[3]
    width = n_pages * page + LANE
    kern = functools.partial(_select_sample_kernel, layer=layer, n_pages=n_pages, page=page)
    return pl.pallas_call(
        kern,
        grid_spec=pltpu.PrefetchScalarGridSpec(
            num_scalar_prefetch=1,
            grid=(n // SB,),
            in_specs=[
                pl.BlockSpec((SB, N_IDX_HEADS, LANE), lambda s, pt: (s, 0, 0)),
                pl.BlockSpec((SB, N_IDX_HEADS, LANE), lambda s, pt: (s, 0, 0)),
                pl.BlockSpec((SB, LANE), lambda s, pt: (s, 0)),
                pl.BlockSpec(memory_space=pl.ANY),
            ],
            out_specs=pl.BlockSpec((SB, width), lambda s, pt: (s, 0)),
            scratch_shapes=[
                pltpu.VMEM((2, SB, IDX_DIM, n_pages * page), F32),
                pltpu.VMEM((SB, width), F32),
                pltpu.SemaphoreType.DMA((2,)),
            ],
        ),
        out_shape=jax.ShapeDtypeStruct((n, width), F32),
        compiler_params=_params(("arbitrary",)),
        name="dsa_select_sample",
    )(page_table, qi3, wbc, sn, cache_kidx_t)


def _attend_sample_kernel(pt_ref, q_ref, kn_ref, vn_ref, bias_ref, ck_ref, cv_ref, o_ref,
                          kbuf_ref, vbuf_ref, ksem_ref, vsem_ref, *, layer, n_pages, page):
    b = pl.program_id(0)
    slot = b % 2
    past = n_pages * page

    def copies(sample, sl):
        return (_page_copies(pt_ref, ck_ref, layer, kbuf_ref, ksem_ref, sl, sample, 1, n_pages)
                + _page_copies(pt_ref, cv_ref, layer, vbuf_ref, vsem_ref, sl, sample, 1, n_pages))

    @pl.when(b == 0)
    def _():
        for cp in copies(0, 0):
            cp.start()

    @pl.when(b + 1 < pl.num_programs(0))
    def _():
        for cp in copies(b + 1, 1 - slot):
            cp.start()

    for cp in copies(b, slot):
        cp.wait()

    qb = q_ref[...].astype(BF16)
    qr = qb.astype(F32)
    bias = bias_ref[...]
    head = lax.broadcasted_iota(I32, (N_HEADS, HEAD_DIM), 0)
    rows = [pl.ds(g, past, stride=N_KV_HEADS) for g in range(N_KV_HEADS)]
    s = [_dot_nt(qb, kbuf_ref[slot, 0, rows[g], :].astype(BF16)) + bias[:, :past] for g in range(N_KV_HEADS)]
    p, p_new, denom = [], [], []
    for g in range(N_KV_HEADS):
        k_new = kn_ref[g:g + 1, :].astype(BF16).astype(F32)
        s_new = jnp.sum(qr * k_new, axis=1, keepdims=True) + bias[:, past:past + 1]
        m = jnp.maximum(jnp.max(s[g], axis=1, keepdims=True), s_new)
        p.append(jnp.exp(s[g] - m))
        p_new.append(jnp.exp(s_new - m))
        denom.append(jnp.sum(p[g], axis=1, keepdims=True) + p_new[g])
    og = [_dot(p[g].astype(BF16), vbuf_ref[slot, 0, rows[g], :].astype(BF16)) for g in range(N_KV_HEADS)]
    o = jnp.zeros((N_HEADS, HEAD_DIM), F32)
    for g in range(N_KV_HEADS):
        v_new = vn_ref[g:g + 1, :].astype(BF16).astype(F32)
        o_g = (og[g] + p_new[g].astype(BF16).astype(F32) * v_new) / denom[g]
        o = jnp.where(head // KV_GROUP == g, o_g, o)
    o_ref[...] = o


def _attend_sample_call(page_table, q3, kn3, vn3, bias3, cache_k, cache_v, layer):
    n, n_pages = page_table.shape
    page = cache_k.shape[2] // N_KV_HEADS
    past = n_pages * page
    kern = functools.partial(_attend_sample_kernel, layer=layer, n_pages=n_pages, page=page)
    per = lambda b, pt: (b, 0, 0)
    return pl.pallas_call(
        kern,
        grid_spec=pltpu.PrefetchScalarGridSpec(
            num_scalar_prefetch=1,
            grid=(n,),
            in_specs=[
                pl.BlockSpec((None, N_HEADS, HEAD_DIM), per),
                pl.BlockSpec((None, N_KV_HEADS, HEAD_DIM), per),
                pl.BlockSpec((None, N_KV_HEADS, HEAD_DIM), per),
                pl.BlockSpec((None, 1, past + LANE), per),
                pl.BlockSpec(memory_space=pl.ANY),
                pl.BlockSpec(memory_space=pl.ANY),
            ],
            out_specs=pl.BlockSpec((None, N_HEADS, HEAD_DIM), per),
            scratch_shapes=[
                pltpu.VMEM((2, 1, N_KV_HEADS * past, HEAD_DIM), F32),
                pltpu.VMEM((2, 1, N_KV_HEADS * past, HEAD_DIM), F32),
                pltpu.SemaphoreType.DMA((2,)),
                pltpu.SemaphoreType.DMA((2,)),
            ],
        ),
        out_shape=jax.ShapeDtypeStruct((n, N_HEADS, HEAD_DIM), F32),
        compiler_params=_params(("arbitrary",)),
        name="dsa_attend_sample",
    )(page_table, q3, kn3, vn3, bias3, cache_k, cache_v)


def _out_sample_kernel(x_ref, mod_ref, lng_ref, lnb_ref, o_ref, w_out_ref, y_ref):
    y = _dot(o_ref[...].astype(BF16), w_out_ref[...])
    y_ref[...] = _post_norm(x_ref[...], y, mod_ref[2], lng_ref[...], lnb_ref[...])


def _out_sample_call(x, mod_s, lng, lnb, o, w_out):
    args = (x, mod_s, _row(lng), _row(lnb), o, w_out)
    return pl.pallas_call(
        _out_sample_kernel,
        grid=(1,),
        in_specs=[_resident_spec(a) for a in args],
        out_specs=_full_spec(x.shape),
        out_shape=jax.ShapeDtypeStruct(x.shape, F32),
        compiler_params=_params(("arbitrary",)),
        name="dsa_out_sample",
    )(*_arrays(args))


def _pack_proj_weight(w_in):
    d = w_in.shape[0]
    o_qi = Q_COLS + 2 * KV_COLS
    o_ki = o_qi + QI_COLS
    o_wi = o_ki + IDX_DIM
    qi = w_in[:, o_qi:o_ki].reshape(d, N_IDX_HEADS, IDX_DIM)
    qi = jnp.pad(qi, ((0, 0), (0, 0), (0, LANE - IDX_DIM))).reshape(d, QI_PAD)
    ki = jnp.pad(w_in[:, o_ki:o_wi], ((0, 0), (0, LANE - IDX_DIM)))
    wi = jnp.pad(w_in[:, o_wi:], ((0, 0), (0, LANE - N_IDX_HEADS)))
    return jnp.concatenate([w_in[:, :o_qi], qi, ki, wi], axis=1).astype(BF16)


def _pack_proj_weight_t(w_in):
    o_v = Q_COLS + KV_COLS
    o_wi = Q_COLS + 2 * KV_COLS + QI_COLS + IDX_DIM
    wt = jnp.concatenate([w_in[:, o_v:o_v + KV_COLS], w_in[:, o_wi:]], axis=1).T
    return jnp.pad(wt, ((0, 2 * SUBLANE - N_IDX_HEADS), (0, 0))).astype(BF16)


def kernel(x_prompt, x_sample, cache_k, cache_v, cache_kidx, state_conv, page_table, c_prompt, c_sample,
           w_ada, b_ada, ln_g, ln_b, sgu_w_in, sgu_b_in, sgu_norm_g, sgu_norm_b, sgu_w_s, sgu_b_s, sgu_w_out,
           dsa_w_in, dsa_w_out, ffn_w_up, ffn_conv_w, ffn_conv_b, ffn_w_down):
    bsz, t_p, _ = x_prompt.shape
    n_s = x_sample.shape[0]
    n_phys, page = cache_k.shape[1], cache_k.shape[2]
    past = page_table.shape[1] * page

    c_prompt8 = jnp.pad(c_prompt, ((0, SUBLANE - bsz), (0, 0)))
    mods_s, mods_p = _ada_call(c_sample, c_prompt8, w_ada, b_ada)

    ck = cache_k.reshape(cache_k.shape[0], n_phys, page * N_KV_HEADS, HEAD_DIM)
    cv = cache_v.reshape(cache_v.shape[0], n_phys, page * N_KV_HEADS, HEAD_DIM)
    ckidx_t = jnp.swapaxes(cache_kidx, 2, 3)

    sgu_w_in_b, sgu_w_out_b = sgu_w_in.astype(BF16), sgu_w_out.astype(BF16)
    dsa_w_out_b = dsa_w_out.astype(BF16)
    ffn_w_up_b, ffn_w_down_b = ffn_w_up.astype(BF16), ffn_w_down.astype(BF16)
    n_dsa = DEPTH // 2

    xp = x_prompt
    xs = x_sample.reshape(n_s, D_MODEL)
    kv_stacks = tuple(jnp.zeros((n_dsa, bsz, N_KV_HEADS * t_p, HEAD_DIM), F32) for _ in range(2))
    kip_l, ks_l, vs_l, kis_l, sgu_l, convp_l, convs_l = [], [], [], [], [], [], []
    for i in range(DEPTH):
        j = i // 2
        mod_p, mod_s = (mods_p, i), (mods_s, i)
        if i % 2 == 0:
            w_in = (sgu_w_in_b, j)
            w_out = (sgu_w_out_b, j)
            wtril = jnp.tril(sgu_w_s[j]).astype(BF16)
            bs_full = jnp.repeat(sgu_b_s[j].T, SGU_GROUP_DIM, axis=1)
            ws0 = jnp.repeat(sgu_w_s[j][:, 0, 0], SGU_GROUP_DIM)
            bs0 = jnp.repeat(sgu_b_s[j][:, 0], SGU_GROUP_DIM)
            xp = _sgu_prompt_call(xp, mod_p, ln_g[i, 0], ln_b[i, 0], w_in, sgu_b_in[j], sgu_norm_g[j],
                                  sgu_norm_b[j], wtril, bs_full, w_out)
            xs, v_rows = _sgu_sample_call(xs, mod_s, ln_g[i, 0], ln_b[i, 0], w_in, sgu_b_in[j], sgu_norm_g[j],
                                          sgu_norm_b[j], ws0, bs0, w_out)
            sgu_l.append(v_rows.reshape(n_s, 1, D_SGU))
        else:
            w_proj = _pack_proj_weight(dsa_w_in[j])
            w_out = (dsa_w_out_b, j)
            q, k_stack, v_stack, kb, vt, qi_hm, ki, kib, wit = _proj_prompt_call(
                xp, mod_p, w_proj, _pack_proj_weight_t(dsa_w_in[j]), j, n_dsa, kv_stacks)
            kv_stacks = (k_stack, v_stack)
            xp = _attend_prompt_call(xp, mod_p, ln_g[i, 0], ln_b[i, 0], q, qi_hm, wit, kb, vt, kib, w_out)
            kip_l.append(ki)

            qs, ks_new, vs_new, qis, kis_new, wis, sn = _proj_sample_call(xs, mod_s, w_proj)
            wbc = jnp.broadcast_to(wis[:, :N_IDX_HEADS, None], (n_s, N_IDX_HEADS, LANE))
            bias = _select_sample_call(page_table, qis.reshape(n_s, N_IDX_HEADS, LANE), wbc, sn, ckidx_t, j)
            o = _attend_sample_call(page_table, qs.reshape(n_s, N_HEADS, HEAD_DIM),
                                    ks_new.reshape(n_s, N_KV_HEADS, HEAD_DIM),
                                    vs_new.reshape(n_s, N_KV_HEADS, HEAD_DIM),
                                    bias.reshape(n_s, 1, past + LANE), ck, cv, j)
            xs = _out_sample_call(xs, mod_s, ln_g[i, 0], ln_b[i, 0], o.reshape(n_s, Q_COLS), w_out)
            ks_l.append(ks_new.reshape(n_s, 1, N_KV_HEADS, HEAD_DIM))
            vs_l.append(vs_new.reshape(n_s, 1, N_KV_HEADS, HEAD_DIM))
            kis_l.append(kis_new.reshape(n_s, 1, IDX_DIM))

        w_up = (ffn_w_up_b, i)
        w_down = (ffn_w_down_b, i)
        xp, conv_p = _ffn_prompt_call(xp, mod_p, ln_g[i, 1], ln_b[i, 1], w_up, ffn_conv_w[i], ffn_conv_b[i], w_down)
        past_t = jnp.swapaxes(state_conv[i], 0, 1)
        xs, a_s = _ffn_sample_call(xs, mod_s, ln_g[i, 1], ln_b[i, 1], w_up, ffn_conv_w[i], ffn_conv_b[i], w_down,
                                   past_t)
        convp_l.append(conv_p)
        convs_l.append(jnp.stack([state_conv[i][:, 1], a_s], axis=1))

    return (xp, xs.reshape(n_s, 1, D_MODEL),
            kv_stacks[0].reshape(n_dsa, bsz, t_p, N_KV_HEADS, HEAD_DIM),
            kv_stacks[1].reshape(n_dsa, bsz, t_p, N_KV_HEADS, HEAD_DIM), jnp.stack(kip_l),
            jnp.stack(ks_l), jnp.stack(vs_l), jnp.stack(kis_l),
            jnp.stack(sgu_l), jnp.stack(convp_l), jnp.stack(convs_l))
```
